```python
import math
import jax, jax.numpy as jnp
from jax import lax
import numpy as np


D_MODEL = 1024
BATCH = 8
SEQ = 2048
DEPTH = 2

D_FF = 2816
BRANCH_WIDTH = D_MODEL // 2
CHUNK = 128
GMLP_GROUPS = 4
GMLP_GROUP_DIM = BRANCH_WIDTH // GMLP_GROUPS
DIFF_QK_DIM = 64
DIFF_V_DIM = 2 * DIFF_QK_DIM
DIFF_HEADS = BRANCH_WIDTH // DIFF_V_DIM
Q_BLOCK = 128
S5_GROUP_DIM = 16
S5_GROUPS = BRANCH_WIDTH // S5_GROUP_DIM
S5_STATE = 64
N_BRANCH = 3
LN_EPS = 1e-5
DN_ALPHA = (2 * DEPTH) ** 0.25
DN_BETA = (8 * DEPTH) ** -0.25
QK_COLS = 4 * DIFF_HEADS * DIFF_QK_DIM
COL_SIZES = (2 * BRANCH_WIDTH, QK_COLS, BRANCH_WIDTH, BRANCH_WIDTH, N_BRANCH * D_MODEL)
IN_WIDTH = sum(COL_SIZES)
SPLIT_IDX = tuple(int(v) for v in np.cumsum(COL_SIZES)[:-1])

kernel_name = 'hybrid_gmlp_diffattn_s5_deepnorm'


def layer_norm(x, g, b):
    xf = x.astype(jnp.float32)
    mu = jnp.mean(xf, axis=-1, keepdims=True)
    var = jnp.mean(jnp.square(xf - mu), axis=-1, keepdims=True)
    return ((xf - mu) * lax.rsqrt(var + LN_EPS) * g + b).astype(x.dtype)


def swiglu(x, wg, wu, wd):
    return (jax.nn.silu(x @ wg) * (x @ wu)) @ wd


def gmlp_mixer(z, ln_g, ln_b, ws, bs):
    z = jax.nn.gelu(z)
    u, v = jnp.split(z, 2, axis=-1)
    v = layer_norm(v, ln_g, ln_b)
    bsz, l, _ = v.shape
    v = v.reshape(bsz, l // CHUNK, CHUNK, GMLP_GROUPS, GMLP_GROUP_DIM)
    mask = jnp.tril(jnp.ones((CHUNK, CHUNK), dtype=bool))
    ws = jnp.where(mask, ws, 0)
    s = jnp.einsum('gts,bcsgd->bctgd', ws, v) + bs.T[None, None, :, :, None]
    return u * s.reshape(bsz, l, BRANCH_WIDTH)


def diff_attention(q, k, v, lam, subln_g, lam_init):
    bsz, l = q.shape[0], q.shape[1]
    scale = DIFF_QK_DIM ** -0.5
    causal = jnp.tril(jnp.ones((Q_BLOCK, Q_BLOCK), dtype=bool))
    outs = []
    for i in range(l // Q_BLOCK):
        q0 = i * Q_BLOCK
        kend = q0 + Q_BLOCK
        s = jnp.einsum('bqmhd,bkmhd->bmhqk', q[:, q0:kend], k[:, :kend]).astype(jnp.float32) * scale
        mask = jnp.concatenate([jnp.ones((Q_BLOCK, q0), dtype=bool), causal], axis=1)
        p = jax.nn.softmax(jnp.where(mask, s, -jnp.inf), axis=-1)
        a = p[:, 0] - lam * p[:, 1]
        outs.append(jnp.einsum('bhqk,bkhd->bqhd', a.astype(v.dtype), v[:, :kend]))
    o = jnp.concatenate(outs, axis=1).astype(jnp.float32)
    o = o * lax.rsqrt(jnp.mean(jnp.square(o), axis=-1, keepdims=True) + LN_EPS) * subln_g * (1.0 - lam_init)
    return o.reshape(bsz, l, BRANCH_WIDTH).astype(v.dtype)


def s5_mixer(xc, lam_re, lam_im, log_step, b_re, b_im, c_re, c_im, d, glu_w, glu_b):
    f32 = jnp.float32
    bsz, l, _ = xc.shape
    u = xc.astype(f32).reshape(bsz, l, S5_GROUPS, S5_GROUP_DIM)
    lr = lam_re.astype(f32)
    li = lam_im.astype(f32)
    step = jnp.exp(log_step.astype(f32))[:, None]
    mag = jnp.exp(lr * step)
    ar = mag * jnp.cos(li * step)
    ai = mag * jnp.sin(li * step)
    den = lr * lr + li * li
    cr = ((ar - 1.0) * lr + ai * li) / den
    ci = (ai * lr - (ar - 1.0) * li) / den
    br = b_re.astype(f32)
    bi = b_im.astype(f32)
    bbar_re = cr[..., None] * br - ci[..., None] * bi
    bbar_im = cr[..., None] * bi + ci[..., None] * br
    bu_re = jnp.einsum('blgh,gph->blgp', u, bbar_re)
    bu_im = jnp.einsum('blgh,gph->blgp', u, bbar_im)
    a_re = jnp.broadcast_to(ar, bu_re.shape)
    a_im = jnp.broadcast_to(ai, bu_im.shape)

    def combine(e1, e2):
        a1r, a1i, b1r, b1i = e1
        a2r, a2i, b2r, b2i = e2
        return (a2r * a1r - a2i * a1i,
                a2r * a1i + a2i * a1r,
                a2r * b1r - a2i * b1i + b2r,
                a2r * b1i + a2i * b1r + b2i)

    _, _, hr, hi = lax.associative_scan(combine, (a_re, a_im, bu_re, bu_im), axis=1)
    y = (jnp.einsum('blgp,ghp->blgh', hr, c_re.astype(f32))
         - jnp.einsum('blgp,ghp->blgh', hi, c_im.astype(f32))
         + d.astype(f32) * u)
    y = jax.nn.gelu(y.reshape(bsz, l, BRANCH_WIDTH)).astype(xc.dtype)
    return y * jax.nn.sigmoid(y @ glu_w + glu_b)


def setup_inputs(seed: int = 0) -> dict:
    key = jax.random.key(seed)
    ks = iter(jax.random.split(key, 48))

    def nrm(shape, scale):
        return jax.random.normal(next(ks), shape, jnp.float32) * scale

    def gain(shape):
        return 1.0 + nrm(shape, 0.02)

    L = DEPTH
    W = BRANCH_WIDTH
    G, P, H = S5_GROUPS, S5_STATE, S5_GROUP_DIM
    inp = {}
    inp['x'] = nrm((BATCH, SEQ, D_MODEL), 1.0)
    inp['ffn1_gate'] = nrm((L, D_MODEL, D_FF), D_MODEL ** -0.5)
    inp['ffn1_up'] = nrm((L, D_MODEL, D_FF), D_MODEL ** -0.5)
    inp['ffn1_down'] = nrm((L, D_FF, D_MODEL), D_FF ** -0.5 * DN_BETA)
    inp['ln1_g'] = gain((L, D_MODEL))
    inp['ln1_b'] = nrm((L, D_MODEL), 0.02)
    inp['w_in'] = nrm((L, D_MODEL, IN_WIDTH), D_MODEL ** -0.5)
    inp['gmlp_ln_g'] = gain((L, W))
    inp['gmlp_ln_b'] = nrm((L, W), 0.02)
    inp['gmlp_ws'] = nrm((L, GMLP_GROUPS, CHUNK, CHUNK), 0.5 * CHUNK ** -0.5)
    inp['gmlp_bs'] = 1.0 + nrm((L, GMLP_GROUPS, CHUNK), 0.1)
    inp['diff_lq1'] = nrm((L, DIFF_QK_DIM), 0.1)
    inp['diff_lk1'] = nrm((L, DIFF_QK_DIM), 0.1)
    inp['diff_lq2'] = nrm((L, DIFF_QK_DIM), 0.1)
    inp['diff_lk2'] = nrm((L, DIFF_QK_DIM), 0.1)
    inp['diff_subln_g'] = gain((L, DIFF_V_DIM))
    inp['s5_lambda_re'] = -0.5 + nrm((L, G, P), 0.01)
    inp['s5_lambda_im'] = math.pi * jnp.arange(P, dtype=jnp.float32) + nrm((L, G, P), 0.01)
    inp['s5_log_step'] = jax.random.uniform(next(ks), (L, G), jnp.float32, math.log(1e-3), math.log(1e-1))
    inp['s5_b_re'] = nrm((L, G, P, H), (2 * H) ** -0.5)
    inp['s5_b_im'] = nrm((L, G, P, H), (2 * H) ** -0.5)
    inp['s5_c_re'] = nrm((L, G, H, P), P ** -0.5)
    inp['s5_c_im'] = nrm((L, G, H, P), P ** -0.5)
    inp['s5_d'] = nrm((L, G, H), 0.5)
    inp['s5_glu_w'] = nrm((L, W, W), W ** -0.5)
    inp['s5_glu_b'] = nrm((L, W), 0.02)
    inp['w_branch'] = nrm((L, N_BRANCH, W, D_MODEL), W ** -0.5)
    inp['w_out'] = nrm((L, D_MODEL, D_MODEL), D_MODEL ** -0.5 * DN_BETA)
    inp['ln2_g'] = gain((L, D_MODEL))
    inp['ln2_b'] = nrm((L, D_MODEL), 0.02)
    inp['ffn2_gate'] = nrm((L, D_MODEL, D_FF), D_MODEL ** -0.5)
    inp['ffn2_up'] = nrm((L, D_MODEL, D_FF), D_MODEL ** -0.5)
    inp['ffn2_down'] = nrm((L, D_FF, D_MODEL), D_FF ** -0.5 * DN_BETA)
    inp['ln3_g'] = gain((L, D_MODEL))
    inp['ln3_b'] = nrm((L, D_MODEL), 0.02)
    return inp


def reference(x, ffn1_gate, ffn1_up, ffn1_down, ln1_g, ln1_b, w_in, gmlp_ln_g, gmlp_ln_b,
              gmlp_ws, gmlp_bs, diff_lq1, diff_lk1, diff_lq2, diff_lk2, diff_subln_g,
              s5_lambda_re, s5_lambda_im, s5_log_step, s5_b_re, s5_b_im, s5_c_re, s5_c_im,
              s5_d, s5_glu_w, s5_glu_b, w_branch, w_out, ln2_g, ln2_b,
              ffn2_gate, ffn2_up, ffn2_down, ln3_g, ln3_b):
    bsz, l, _ = x.shape
    for i in range(DEPTH):
        x = layer_norm(DN_ALPHA * x + 0.5 * swiglu(x, ffn1_gate[i], ffn1_up[i], ffn1_down[i]), ln1_g[i], ln1_b[i])

        h = x @ w_in[i]
        z_gmlp, z_qk, z_v, z_s5, z_gate = jnp.split(h, SPLIT_IDX, axis=-1)

        y_a = gmlp_mixer(z_gmlp, gmlp_ln_g[i], gmlp_ln_b[i], gmlp_ws[i], gmlp_bs[i])

        qk = z_qk.reshape(bsz, l, 4, DIFF_HEADS, DIFF_QK_DIM)
        v = z_v.reshape(bsz, l, DIFF_HEADS, DIFF_V_DIM)
        lam_init = 0.8 - 0.6 * math.exp(-0.3 * i)
        f32 = jnp.float32
        lam = (jnp.exp(jnp.sum(diff_lq1[i].astype(f32) * diff_lk1[i].astype(f32)))
               - jnp.exp(jnp.sum(diff_lq2[i].astype(f32) * diff_lk2[i].astype(f32))) + lam_init)
        y_b = diff_attention(qk[:, :, 0:2], qk[:, :, 2:4], v, lam, diff_subln_g[i], lam_init)

        y_c = s5_mixer(z_s5, s5_lambda_re[i], s5_lambda_im[i], s5_log_step[i], s5_b_re[i], s5_b_im[i],
                       s5_c_re[i], s5_c_im[i], s5_d[i], s5_glu_w[i], s5_glu_b[i])

        branches = jnp.stack([y_a, y_b, y_c], axis=2)
        proj = jnp.einsum('blkw,kwd->blkd', branches, w_branch[i])
        gates = jax.nn.sigmoid(z_gate.reshape(bsz, l, N_BRANCH, D_MODEL))
        mix = jnp.sum(gates * proj, axis=2) @ w_out[i]
        x = layer_norm(DN_ALPHA * x + mix, ln2_g[i], ln2_b[i])

        x = layer_norm(DN_ALPHA * x + 0.5 * swiglu(x, ffn2_gate[i], ffn2_up[i], ffn2_down[i]), ln3_g[i], ln3_b[i])
    return x
```

```python
import functools
import math

import jax
import jax.numpy as jnp
from jax import lax
from jax.experimental import pallas as pl
from jax.experimental.pallas import tpu as pltpu

F32 = jnp.float32
BF16 = jnp.bfloat16

CHUNK = 128
GMLP_GROUPS = 4
DIFF_QK_DIM = 64
DIFF_V_DIM = 128
S5_GROUP_DIM = 16
S5_STATE = 64
N_BRANCH = 3
LN_EPS = 1e-5

LANES = 128
SUBLANES = 8
VMEM_LIMIT = 56 * 1024 * 1024

TM = 512
TF = 256
TQ = 256
S5_T = 128


def _layer_norm(y, g, b):
    mu = jnp.mean(y, axis=-1, keepdims=True)
    d = y - mu
    var = jnp.mean(d * d, axis=-1, keepdims=True)
    return d * lax.rsqrt(var + LN_EPS) * g + b


def _dot(a, b):
    return jnp.dot(a, b, preferred_element_type=F32)


def _const_spec(shape):
    zeros = (0,) * len(shape)
    return pl.BlockSpec(shape, lambda *_: zeros, pipeline_mode=pl.Buffered(1))


def _params(*sem):
    return pltpu.CompilerParams(dimension_semantics=sem, vmem_limit_bytes=VMEM_LIMIT)


def _ffn_kernel(x_ref, wg_ref, wu_ref, wd_ref, g_ref, b_ref, o_ref, h_ref, *, alpha):
    x = x_ref[...]
    xb = x.astype(BF16)
    d_ff = wg_ref.shape[1]
    for c in range(d_ff // TF):
        cols = slice(c * TF, (c + 1) * TF)
        gate = _dot(xb, wg_ref[:, cols])
        up = _dot(xb, wu_ref[:, cols])
        h_ref[:, cols] = (gate * jax.nn.sigmoid(gate) * up).astype(BF16)
    out = _dot(h_ref[...], wd_ref[...])
    o_ref[...] = _layer_norm(alpha * x + 0.5 * out, g_ref[...], b_ref[...])


def _ffn(x, wg, wu, wd, g, b, alpha):
    n, d = x.shape
    d_ff = wg.shape[1]
    row = pl.BlockSpec((TM, d), lambda i: (i, 0))
    return pl.pallas_call(
        functools.partial(_ffn_kernel, alpha=alpha),
        grid=(n // TM,),
        in_specs=[row, _const_spec((d, d_ff)), _const_spec((d, d_ff)), _const_spec((d_ff, d)),
                  _const_spec((1, d)), _const_spec((1, d))],
        out_specs=row,
        out_shape=jax.ShapeDtypeStruct((n, d), F32),
        scratch_shapes=[pltpu.VMEM((TM, d_ff), BF16)],
        compiler_params=_params("parallel"),
        name="ffn",
    )(x, wg.astype(BF16), wu.astype(BF16), wd.astype(BF16), g.reshape(1, d), b.reshape(1, d))


def _in_kernel(x_ref, w_ref, lng_ref, lnb_ref, ws_ref, bs_ref,
               ya_ref, q_ref, k_ref, v_ref, zs_ref, *, width, qk_scale):
    xb = x_ref[...].astype(BF16)
    tm = xb.shape[0]
    w2 = 2 * width

    zg = jax.nn.gelu(_dot(xb, w_ref[:, 0:w2]))
    u = zg[:, :width]
    vn = _layer_norm(zg[:, width:], lng_ref[...], lnb_ref[...]).astype(BF16)
    t_idx = lax.broadcasted_iota(jnp.int32, (CHUNK, CHUNK), 0)
    s_idx = lax.broadcasted_iota(jnp.int32, (CHUNK, CHUNK), 1)
    causal = s_idx <= t_idx
    gd = width // GMLP_GROUPS
    for g in range(GMLP_GROUPS):
        ws = jnp.where(causal, ws_ref[g], 0.0).astype(BF16)
        cols = slice(g * gd, (g + 1) * gd)
        for c in range(tm // CHUNK):
            rows = slice(c * CHUNK, (c + 1) * CHUNK)
            s = _dot(ws, vn[rows, cols]) + bs_ref[g]
            ya_ref[rows, cols] = (u[rows, cols] * s).astype(BF16)

    zqk = _dot(xb, w_ref[:, w2:2 * w2])
    q_ref[...] = (zqk[:, :width] * qk_scale).astype(BF16)
    k_ref[...] = zqk[:, width:].astype(BF16)
    v_ref[...] = _dot(xb, w_ref[:, 2 * w2:2 * w2 + width]).astype(BF16)
    zs_ref[...] = _dot(xb, w_ref[:, 2 * w2 + width:2 * w2 + 2 * width])


def _in_proj(x, w, lng, lnb, ws, bs, bsz, seq, width):
    n, d = x.shape
    lt = seq // TM
    row = lambda wd_: pl.BlockSpec((TM, wd_), lambda i: (i, 0))
    out_bf = jax.ShapeDtypeStruct((n, width), BF16)
    return pl.pallas_call(
        functools.partial(_in_kernel, width=width, qk_scale=DIFF_QK_DIM ** -0.5),
        grid=(n // TM,),
        in_specs=[row(d), _const_spec(w.shape), _const_spec((1, width)), _const_spec((1, width)),
                  _const_spec(ws.shape), _const_spec(bs.shape)],
        out_specs=[row(width), row(width), row(width), row(width),
                   pl.BlockSpec((TM, width), lambda i: (i % lt, i // lt))],
        out_shape=[out_bf, out_bf, out_bf, out_bf,
                   jax.ShapeDtypeStruct((seq, bsz * width), F32)],
        compiler_params=_params("parallel"),
        name="in_proj_gmlp",
    )(x, w, lng.reshape(1, width), lnb.reshape(1, width), ws, bs)


def _attn_kernel(lam_ref, q_ref, k_ref, v_ref, g_ref, o_ref, acc1_ref, acc2_ref, *, out_scale):
    i = pl.program_id(2)
    q = q_ref[...]
    lane = lax.broadcasted_iota(jnp.int32, q.shape, 1)
    zero = jnp.zeros_like(q)
    q1 = jnp.where(lane < DIFF_QK_DIM, q, zero)
    q2 = jnp.where(lane >= DIFF_QK_DIM, q, zero)
    acc1_ref[...] = jnp.zeros_like(acc1_ref)
    acc2_ref[...] = jnp.zeros_like(acc2_ref)
    nt = (((1,), (1,)), ((), ()))

    def one_map(qm, kj, vj, m, l, acc_ref, mask):
        s = lax.dot_general(qm, kj, nt, preferred_element_type=F32)
        if mask is not None:
            s = jnp.where(mask, s, -jnp.inf)
        m_new = jnp.maximum(m, jnp.max(s, axis=-1, keepdims=True))
        p = jnp.exp(s - m_new)
        corr = jnp.exp(m - m_new)
        l_new = corr * l + jnp.sum(p, axis=-1, keepdims=True)
        acc_ref[...] = corr * acc_ref[...] + _dot(p.astype(BF16), vj)
        return m_new, l_new

    def block(j, carry, mask):
        m1, l1, m2, l2 = carry
        r = pl.multiple_of(j * TQ, TQ)
        kj = k_ref[pl.ds(r, TQ), :]
        vj = v_ref[pl.ds(r, TQ), :]
        m1, l1 = one_map(q1, kj, vj, m1, l1, acc1_ref, mask)
        m2, l2 = one_map(q2, kj, vj, m2, l2, acc2_ref, mask)
        return m1, l1, m2, l2

    neg = jnp.full((TQ, 1), -jnp.inf, F32)
    zer = jnp.zeros((TQ, 1), F32)
    carry = lax.fori_loop(0, i, lambda j, c: block(j, c, None), (neg, zer, neg, zer))
    row = lax.broadcasted_iota(jnp.int32, (TQ, TQ), 0)
    col = lax.broadcasted_iota(jnp.int32, (TQ, TQ), 1)
    _, l1, _, l2 = block(i, carry, col <= row)

    o = acc1_ref[...] / l1 - lam_ref[0] * (acc2_ref[...] / l2)
    o = o * lax.rsqrt(jnp.mean(o * o, axis=-1, keepdims=True) + LN_EPS) * g_ref[...] * out_scale
    o_ref[...] = o.astype(BF16)


def _diff_attention(q, k, v, lam, subln_g, out_scale, bsz, seq):
    n, width = q.shape
    heads = width // DIFF_V_DIM
    qt = seq // TQ
    q_spec = pl.BlockSpec((TQ, DIFF_V_DIM), lambda b, h, i: (b * qt + i, h))
    kv_spec = pl.BlockSpec((seq, DIFF_V_DIM), lambda b, h, i: (b, h))
    return pl.pallas_call(
        functools.partial(_attn_kernel, out_scale=out_scale),
        grid=(bsz, heads, qt),
        in_specs=[pl.BlockSpec(memory_space=pltpu.SMEM), q_spec, kv_spec, kv_spec,
                  pl.BlockSpec((1, DIFF_V_DIM), lambda b, h, i: (0, 0))],
        out_specs=q_spec,
        out_shape=jax.ShapeDtypeStruct((n, width), BF16),
        scratch_shapes=[pltpu.VMEM((TQ, DIFF_V_DIM), F32), pltpu.VMEM((TQ, DIFF_V_DIM), F32)],
        compiler_params=_params("parallel", "parallel", "arbitrary"),
        name="diff_attention",
    )(lam.reshape(1), q, k, v, subln_g.reshape(1, DIFF_V_DIM))


def _s5_kernel(u_ref, bcat_ref, ccat_ref, ar_ref, ai_ref, d_ref, gw_ref, gb_ref,
               o_ref, h_ref, st_ref, y_ref, *, bsz):
    @pl.when(pl.program_id(0) == 0)
    def _():
        st_ref[...] = jnp.zeros_like(st_ref)

    u = u_ref[...]
    ub = u.astype(BF16)
    n_blk = bcat_ref.shape[0]
    ns = h_ref.shape[1] // 2
    steps = u.shape[0] // bsz
    for j in range(n_blk):
        h_ref[...] = _dot(ub[:, j * LANES:(j + 1) * LANES], bcat_ref[j])
        ar = ar_ref[j]
        ai = ai_ref[j]

        def step(t, carry):
            hr, hi = carry
            r = pl.multiple_of(t * bsz, bsz)
            nhr = ar * hr - ai * hi + h_ref[pl.ds(r, bsz), 0:ns]
            nhi = ar * hi + ai * hr + h_ref[pl.ds(r, bsz), ns:2 * ns]
            h_ref[pl.ds(r, bsz), 0:ns] = nhr
            h_ref[pl.ds(r, bsz), ns:2 * ns] = nhi
            return nhr, nhi

        hr, hi = lax.fori_loop(0, steps, step, (st_ref[j, 0], st_ref[j, 1]), unroll=8)
        st_ref[j, 0] = hr
        st_ref[j, 1] = hi
        y_ref[:, j * LANES:(j + 1) * LANES] = _dot(h_ref[...].astype(BF16), ccat_ref[j])

    y = jax.nn.gelu(y_ref[...] + d_ref[...] * u)
    gate = jax.nn.sigmoid(_dot(y.astype(BF16), gw_ref[...]) + gb_ref[...])
    o_ref[...] = (y * gate).astype(BF16)


def _s5_params(lam_re, lam_im, log_step, b_re, b_im, c_re, c_im, d, bsz):
    g, p = lam_re.shape
    h = b_re.shape[-1]
    gl = LANES // h
    nb = g // gl
    lr = lam_re.astype(F32)
    li = lam_im.astype(F32)
    step = jnp.exp(log_step.astype(F32))[:, None]
    mag = jnp.exp(lr * step)
    ar = mag * jnp.cos(li * step)
    ai = mag * jnp.sin(li * step)
    den = lr * lr + li * li
    cr = ((ar - 1.0) * lr + ai * li) / den
    ci = (ai * lr - (ar - 1.0) * li) / den
    br = b_re.astype(F32)
    bi = b_im.astype(F32)
    bbar_re = cr[..., None] * br - ci[..., None] * bi
    bbar_im = cr[..., None] * bi + ci[..., None] * br
    eye = jnp.eye(gl, dtype=F32)

    def in_blocks(bb):
        t = bb.transpose(0, 2, 1).reshape(nb, gl, h, p)
        return jnp.einsum('jghp,gk->jghkp', t, eye).reshape(nb, gl * h, gl * p)

    def out_blocks(cc):
        t = cc.astype(F32).transpose(0, 2, 1).reshape(nb, gl, p, h)
        return jnp.einsum('jgph,gk->jgpkh', t, eye).reshape(nb, gl * p, gl * h)

    bcat = jnp.concatenate([in_blocks(bbar_re), in_blocks(bbar_im)], axis=-1).astype(BF16)
    ccat = jnp.concatenate([out_blocks(c_re), -out_blocks(c_im)], axis=1).astype(BF16)
    bro = lambda a: jnp.broadcast_to(a.reshape(nb, 1, gl * p), (nb, bsz, gl * p))
    return bcat, ccat, bro(ar), bro(ai), d.astype(F32).reshape(1, g * h)


def _s5(u, bcat, ccat, ar, ai, d, glu_w, glu_b, bsz):
    n, width = u.shape
    rows = S5_T * bsz
    nb, _, ns2 = bcat.shape
    row = pl.BlockSpec((rows, width), lambda t: (t, 0))
    return pl.pallas_call(
        functools.partial(_s5_kernel, bsz=bsz),
        grid=(n // rows,),
        in_specs=[row, _const_spec(bcat.shape), _const_spec(ccat.shape), _const_spec(ar.shape),
                  _const_spec(ai.shape), _const_spec((1, width)), _const_spec((width, width)),
                  _const_spec((1, width))],
        out_specs=row,
        out_shape=jax.ShapeDtypeStruct((n, width), BF16),
        scratch_shapes=[pltpu.VMEM((rows, ns2), F32),
                        pltpu.VMEM((nb, 2, bsz, ns2 // 2), F32),
                        pltpu.VMEM((rows, width), F32)],
        compiler_params=_params("arbitrary"),
        name="s5_scan",
    )(u, bcat, ccat, ar, ai, d, glu_w.astype(BF16), glu_b.reshape(1, width))


def _mix_kernel(x_ref, ya_ref, yb_ref, yc_ref, wg_ref, wb_ref, wo_ref, g_ref, b_ref, o_ref, *, alpha):
    x = x_ref[...]
    xb = x.astype(BF16)
    d = x.shape[1]
    mix = None
    for k, y_ref in enumerate((ya_ref, yb_ref, yc_ref)):
        gate = jax.nn.sigmoid(_dot(xb, wg_ref[:, k * d:(k + 1) * d]))
        term = gate * _dot(y_ref[...], wb_ref[k])
        mix = term if mix is None else mix + term
    out = _dot(mix.astype(BF16), wo_ref[...])
    o_ref[...] = _layer_norm(alpha * x + out, g_ref[...], b_ref[...])


def _mix(x, ya, yb, yc_tm, w_gate, w_branch, w_out, g, b, alpha, seq):
    n, d = x.shape
    width = ya.shape[1]
    lt = seq // TM
    row = lambda wd_: pl.BlockSpec((TM, wd_), lambda i: (i, 0))
    return pl.pallas_call(
        functools.partial(_mix_kernel, alpha=alpha),
        grid=(n // TM,),
        in_specs=[row(d), row(width), row(width),
                  pl.BlockSpec((TM, width), lambda i: (i % lt, i // lt)),
                  _const_spec(w_gate.shape), _const_spec(w_branch.shape), _const_spec(w_out.shape),
                  _const_spec((1, d)), _const_spec((1, d))],
        out_specs=row(d),
        out_shape=jax.ShapeDtypeStruct((n, d), F32),
        compiler_params=_params("parallel"),
        name="branch_mix",
    )(x, ya, yb, yc_tm, w_gate, w_branch, w_out, g.reshape(1, d), b.reshape(1, d))


def _head_major_qk(w_qk):
    d, cols = w_qk.shape
    heads = cols // (4 * DIFF_QK_DIM)
    w = w_qk.reshape(d, 2, 2, heads, DIFF_QK_DIM)
    return w.transpose(0, 1, 3, 2, 4).reshape(d, cols)


def kernel(x, ffn1_gate, ffn1_up, ffn1_down, ln1_g, ln1_b, w_in, gmlp_ln_g, gmlp_ln_b, gmlp_ws, gmlp_bs, diff_lq1, diff_lk1, diff_lq2, diff_lk2, diff_subln_g, s5_lambda_re, s5_lambda_im, s5_log_step, s5_b_re, s5_b_im, s5_c_re, s5_c_im, s5_d, s5_glu_w, s5_glu_b, w_branch, w_out, ln2_g, ln2_b, ffn2_gate, ffn2_up, ffn2_down, ln3_g, ln3_b):
    bsz, seq, d = x.shape
    depth = w_in.shape[0]
    width = w_branch.shape[2]
    alpha = (2 * depth) ** 0.25
    n = bsz * seq
    gate0 = 4 * width + 2 * width
    xf = x.reshape(n, d)
    for i in range(depth):
        xf = _ffn(xf, ffn1_gate[i], ffn1_up[i], ffn1_down[i], ln1_g[i], ln1_b[i], alpha)

        w = w_in[i]
        w_front = jnp.concatenate(
            [w[:, :2 * width], _head_major_qk(w[:, 2 * width:4 * width]), w[:, 4 * width:gate0]],
            axis=1).astype(BF16)
        bs_full = jnp.broadcast_to(gmlp_bs[i][:, :, None], (GMLP_GROUPS, CHUNK, width // GMLP_GROUPS))
        ya, q, k, v, zs = _in_proj(xf, w_front, gmlp_ln_g[i], gmlp_ln_b[i], gmlp_ws[i], bs_full,
                                   bsz, seq, width)

        lam_init = 0.8 - 0.6 * math.exp(-0.3 * i)
        lam = (jnp.exp(jnp.sum(diff_lq1[i].astype(F32) * diff_lk1[i].astype(F32)))
               - jnp.exp(jnp.sum(diff_lq2[i].astype(F32) * diff_lk2[i].astype(F32))) + lam_init)
        yb = _diff_attention(q, k, v, lam, diff_subln_g[i], 1.0 - lam_init, bsz, seq)

        s5p = _s5_params(s5_lambda_re[i], s5_lambda_im[i], s5_log_step[i], s5_b_re[i], s5_b_im[i],
                         s5_c_re[i], s5_c_im[i], s5_d[i], bsz)
        yc = _s5(zs.reshape(seq * bsz, width), *s5p, s5_glu_w[i], s5_glu_b[i], bsz)

        xf = _mix(xf, ya, yb, yc.reshape(seq, bsz * width), w[:, gate0:].astype(BF16),
                  w_branch[i].astype(BF16), w_out[i].astype(BF16), ln2_g[i], ln2_b[i], alpha, seq)

        xf = _ffn(xf, ffn2_gate[i], ffn2_up[i], ffn2_down[i], ln3_g[i], ln3_b[i], alpha)
    return xf.reshape(bsz, seq, d)
```

```python
import functools
import math

import jax
import jax.numpy as jnp
from jax import lax
from jax.experimental import pallas as pl
from jax.experimental.pallas import tpu as pltpu

F32 = jnp.float32
BF16 = jnp.bfloat16

CHUNK = 128
GMLP_GROUPS = 4
DIFF_QK_DIM = 64
DIFF_V_DIM = 128
S5_GROUP_DIM = 16
S5_STATE = 64
N_BRANCH = 3
LN_EPS = 1e-5

LANES = 128
SUBLANES = 8
VMEM_LIMIT = 56 * 1024 * 1024

TM = 512
TF = 256
TQ = 512
S5_T = 128


def _layer_norm(y, g, b):
    mu = jnp.mean(y, axis=-1, keepdims=True)
    d = y - mu
    var = jnp.mean(d * d, axis=-1, keepdims=True)
    return d * lax.rsqrt(var + LN_EPS) * g + b


def _dot(a, b):
    return jnp.dot(a, b, preferred_element_type=F32)


def _const_spec(shape):
    zeros = (0,) * len(shape)
    return pl.BlockSpec(shape, lambda *_: zeros, pipeline_mode=pl.Buffered(1))


def _params(*sem):
    return pltpu.CompilerParams(dimension_semantics=sem, vmem_limit_bytes=VMEM_LIMIT)


def _ffn_kernel(x_ref, wg_ref, wu_ref, wd_ref, g_ref, b_ref, o_ref, h_ref, *, alpha):
    x = x_ref[...]
    xb = x.astype(BF16)
    d_ff = wg_ref.shape[1]
    for c in range(d_ff // TF):
        cols = slice(c * TF, (c + 1) * TF)
        gate = _dot(xb, wg_ref[:, cols])
        up = _dot(xb, wu_ref[:, cols])
        h_ref[:, cols] = (gate * jax.nn.sigmoid(gate) * up).astype(BF16)
    out = _dot(h_ref[...], wd_ref[...])
    o_ref[...] = _layer_norm(alpha * x + 0.5 * out, g_ref[...], b_ref[...])


def _ffn(x, wg, wu, wd, g, b, alpha):
    n, d = x.shape
    d_ff = wg.shape[1]
    row = pl.BlockSpec((TM, d), lambda i: (i, 0))
    return pl.pallas_call(
        functools.partial(_ffn_kernel, alpha=alpha),
        grid=(n // TM,),
        in_specs=[row, _const_spec((d, d_ff)), _const_spec((d, d_ff)), _const_spec((d_ff, d)),
                  _const_spec((1, d)), _const_spec((1, d))],
        out_specs=row,
        out_shape=jax.ShapeDtypeStruct((n, d), F32),
        scratch_shapes=[pltpu.VMEM((TM, d_ff), BF16)],
        compiler_params=_params("parallel"),
        name="ffn",
    )(x, wg.astype(BF16), wu.astype(BF16), wd.astype(BF16), g.reshape(1, d), b.reshape(1, d))


def _in_kernel(x_ref, w_ref, wvt_ref, lng_ref, lnb_ref, ws_ref, bs_ref,
               ya_ref, q_ref, k_ref, vt_ref, zs_ref, *, width, qk_scale):
    xb = x_ref[...].astype(BF16)
    tm = xb.shape[0]
    w2 = 2 * width

    zg = jax.nn.gelu(_dot(xb, w_ref[:, 0:w2]))
    u = zg[:, :width]
    vn = _layer_norm(zg[:, width:], lng_ref[...], lnb_ref[...]).astype(BF16)
    t_idx = lax.broadcasted_iota(jnp.int32, (CHUNK, CHUNK), 0)
    s_idx = lax.broadcasted_iota(jnp.int32, (CHUNK, CHUNK), 1)
    causal = s_idx <= t_idx
    gd = width // GMLP_GROUPS
    for g in range(GMLP_GROUPS):
        ws = jnp.where(causal, ws_ref[g], 0.0).astype(BF16)
        cols = slice(g * gd, (g + 1) * gd)
        for c in range(tm // CHUNK):
            rows = slice(c * CHUNK, (c + 1) * CHUNK)
            s = _dot(ws, vn[rows, cols]) + bs_ref[g]
            ya_ref[rows, cols] = (u[rows, cols] * s).astype(BF16)

    zqk = _dot(xb, w_ref[:, w2:2 * w2])
    q_ref[...] = (zqk[:, :width] * qk_scale).astype(BF16)
    k_ref[...] = zqk[:, width:].astype(BF16)
    nt = (((1,), (1,)), ((), ()))
    vt_ref[...] = lax.dot_general(wvt_ref[...], xb, nt, preferred_element_type=F32).astype(BF16)
    zs_ref[...] = _dot(xb, w_ref[:, 2 * w2:2 * w2 + width])


def _in_proj(x, w, wvt, lng, lnb, ws, bs, bsz, seq, width):
    n, d = x.shape
    lt = seq // TM
    row = lambda wd_: pl.BlockSpec((TM, wd_), lambda i: (i, 0))
    out_bf = jax.ShapeDtypeStruct((n, width), BF16)
    return pl.pallas_call(
        functools.partial(_in_kernel, width=width, qk_scale=math.log2(math.e) * DIFF_QK_DIM ** -0.5),
        grid=(n // TM,),
        in_specs=[row(d), _const_spec(w.shape), _const_spec(wvt.shape), _const_spec((1, width)),
                  _const_spec((1, width)), _const_spec(ws.shape), _const_spec(bs.shape)],
        out_specs=[row(width), row(width), row(width),
                   pl.BlockSpec((width, TM), lambda i: (i // lt, i % lt)),
                   pl.BlockSpec((TM, width), lambda i: (i % lt, i // lt))],
        out_shape=[out_bf, out_bf, out_bf,
                   jax.ShapeDtypeStruct((bsz * width, seq), BF16),
                   jax.ShapeDtypeStruct((seq, bsz * width), F32)],
        compiler_params=_params("parallel"),
        name="in_proj_gmlp",
    )(x, w, wvt, lng.reshape(1, width), lnb.reshape(1, width), ws, bs)


def _attn_kernel(lam_ref, q_ref, k_ref, vt_ref, g_ref, o_ref, acc1_ref, acc2_ref, *, out_scale):
    i = pl.program_id(2)
    q = q_ref[...]
    lane = lax.broadcasted_iota(jnp.int32, q.shape, 1)
    zero = jnp.zeros_like(q)
    q1 = jnp.where(lane < DIFF_QK_DIM, q, zero)
    q2 = jnp.where(lane >= DIFF_QK_DIM, q, zero)
    acc1_ref[...] = jnp.zeros_like(acc1_ref)
    acc2_ref[...] = jnp.zeros_like(acc2_ref)
    nt = (((1,), (1,)), ((), ()))

    def softmax_step(s, m, l, mask):
        if mask is not None:
            s = jnp.where(mask, s, -jnp.inf)
        m_new = jnp.maximum(m, jnp.max(s, axis=0, keepdims=True))
        p = jnp.exp2(s - m_new)
        corr = jnp.exp2(m - m_new)
        l_new = corr * l + jnp.sum(p, axis=0, keepdims=True)
        return m_new, l_new, corr, p.astype(BF16)

    def block(j, carry, mask):
        m1, l1, m2, l2 = carry
        r = pl.multiple_of(j * TQ, TQ)
        kj = k_ref[pl.ds(r, TQ), :]
        vtj = vt_ref[:, pl.ds(r, TQ)]
        s1 = lax.dot_general(kj, q1, nt, preferred_element_type=F32)
        s2 = lax.dot_general(kj, q2, nt, preferred_element_type=F32)
        m1, l1, corr1, p1 = softmax_step(s1, m1, l1, mask)
        m2, l2, corr2, p2 = softmax_step(s2, m2, l2, mask)
        pv1 = _dot(vtj, p1)
        pv2 = _dot(vtj, p2)
        acc1_ref[...] = corr1 * acc1_ref[...] + pv1
        acc2_ref[...] = corr2 * acc2_ref[...] + pv2
        return m1, l1, m2, l2

    neg = jnp.full((1, TQ), -jnp.inf, F32)
    zer = jnp.zeros((1, TQ), F32)
    carry = lax.fori_loop(0, i, lambda j, c: block(j, c, None), (neg, zer, neg, zer))
    key = lax.broadcasted_iota(jnp.int32, (TQ, TQ), 0)
    qry = lax.broadcasted_iota(jnp.int32, (TQ, TQ), 1)
    _, l1, _, l2 = block(i, carry, key <= qry)

    ot = acc1_ref[...] * (1.0 / l1) - acc2_ref[...] * (lam_ref[0] / l2)
    o = ot.T
    o = o * lax.rsqrt(jnp.mean(o * o, axis=-1, keepdims=True) + LN_EPS) * g_ref[...] * out_scale
    o_ref[...] = o.astype(BF16)


def _diff_attention(q, k, vt, lam, subln_g, out_scale, bsz, seq):
    n, width = q.shape
    heads = width // DIFF_V_DIM
    qt = seq // TQ
    q_spec = pl.BlockSpec((TQ, DIFF_V_DIM), lambda b, h, i: (b * qt + i, h))
    k_spec = pl.BlockSpec((seq, DIFF_V_DIM), lambda b, h, i: (b, h))
    vt_spec = pl.BlockSpec((DIFF_V_DIM, seq), lambda b, h, i: (b * heads + h, 0))
    return pl.pallas_call(
        functools.partial(_attn_kernel, out_scale=out_scale),
        grid=(bsz, heads, qt),
        in_specs=[pl.BlockSpec(memory_space=pltpu.SMEM), q_spec, k_spec, vt_spec,
                  pl.BlockSpec((1, DIFF_V_DIM), lambda b, h, i: (0, 0))],
        out_specs=q_spec,
        out_shape=jax.ShapeDtypeStruct((n, width), BF16),
        scratch_shapes=[pltpu.VMEM((DIFF_V_DIM, TQ), F32), pltpu.VMEM((DIFF_V_DIM, TQ), F32)],
        compiler_params=_params("parallel", "parallel", "arbitrary"),
        name="diff_attention",
    )(lam.reshape(1), q, k, vt, subln_g.reshape(1, DIFF_V_DIM))


def _s5_kernel(u_ref, bcat_ref, ccat_ref, ar_ref, ai_ref, d_ref, gw_ref, gb_ref,
               o_ref, h_ref, st_ref, y_ref, *, bsz):
    @pl.when(pl.program_id(0) == 0)
    def _():
        st_ref[...] = jnp.zeros_like(st_ref)

    u = u_ref[...]
    ub = u.astype(BF16)
    n_blk = bcat_ref.shape[0]
    ns = h_ref.shape[1] // 2
    steps = u.shape[0] // bsz
    for j in range(n_blk):
        h_ref[...] = _dot(ub[:, j * LANES:(j + 1) * LANES], bcat_ref[j])
        ar = ar_ref[j]
        ai = ai_ref[j]

        def step(t, carry):
            hr, hi = carry
            r = pl.multiple_of(t * bsz, bsz)
            nhr = ar * hr - ai * hi + h_ref[pl.ds(r, bsz), 0:ns]
            nhi = ar * hi + ai * hr + h_ref[pl.ds(r, bsz), ns:2 * ns]
            h_ref[pl.ds(r, bsz), 0:ns] = nhr
            h_ref[pl.ds(r, bsz), ns:2 * ns] = nhi
            return nhr, nhi

        hr, hi = lax.fori_loop(0, steps, step, (st_ref[j, 0], st_ref[j, 1]), unroll=8)
        st_ref[j, 0] = hr
        st_ref[j, 1] = hi
        y_ref[:, j * LANES:(j + 1) * LANES] = _dot(h_ref[...].astype(BF16), ccat_ref[j])

    y = jax.nn.gelu(y_ref[...] + d_ref[...] * u)
    gate = jax.nn.sigmoid(_dot(y.astype(BF16), gw_ref[...]) + gb_ref[...])
    o_ref[...] = (y * gate).astype(BF16)


def _s5_params(lam_re, lam_im, log_step, b_re, b_im, c_re, c_im, d, bsz):
    g, p = lam_re.shape
    h = b_re.shape[-1]
    gl = LANES // h
    nb = g // gl
    lr = lam_re.astype(F32)
    li = lam_im.astype(F32)
    step = jnp.exp(log_step.astype(F32))[:, None]
    mag = jnp.exp(lr * step)
    ar = mag * jnp.cos(li * step)
    ai = mag * jnp.sin(li * step)
    den = lr * lr + li * li
    cr = ((ar - 1.0) * lr + ai * li) / den
    ci = (ai * lr - (ar - 1.0) * li) / den
    br = b_re.astype(F32)
    bi = b_im.astype(F32)
    bbar_re = cr[..., None] * br - ci[..., None] * bi
    bbar_im = cr[..., None] * bi + ci[..., None] * br
    eye = jnp.eye(gl, dtype=F32)

    def in_blocks(bb):
        t = bb.transpose(0, 2, 1).reshape(nb, gl, h, p)
        return jnp.einsum('jghp,gk->jghkp', t, eye).reshape(nb, gl * h, gl * p)

    def out_blocks(cc):
        t = cc.astype(F32).transpose(0, 2, 1).reshape(nb, gl, p, h)
        return jnp.einsum('jgph,gk->jgpkh', t, eye).reshape(nb, gl * p, gl * h)

    bcat = jnp.concatenate([in_blocks(bbar_re), in_blocks(bbar_im)], axis=-1).astype(BF16)
    ccat = jnp.concatenate([out_blocks(c_re), -out_blocks(c_im)], axis=1).astype(BF16)
    bro = lambda a: jnp.broadcast_to(a.reshape(nb, 1, gl * p), (nb, bsz, gl * p))
    return bcat, ccat, bro(ar), bro(ai), d.astype(F32).reshape(1, g * h)


def _s5(u, bcat, ccat, ar, ai, d, glu_w, glu_b, bsz):
    n, width = u.shape
    rows = S5_T * bsz
    nb, _, ns2 = bcat.shape
    row = pl.BlockSpec((rows, width), lambda t: (t, 0))
    return pl.pallas_call(
        functools.partial(_s5_kernel, bsz=bsz),
        grid=(n // rows,),
        in_specs=[row, _const_spec(bcat.shape), _const_spec(ccat.shape), _const_spec(ar.shape),
                  _const_spec(ai.shape), _const_spec((1, width)), _const_spec((width, width)),
                  _const_spec((1, width))],
        out_specs=row,
        out_shape=jax.ShapeDtypeStruct((n, width), BF16),
        scratch_shapes=[pltpu.VMEM((rows, ns2), F32),
                        pltpu.VMEM((nb, 2, bsz, ns2 // 2), F32),
                        pltpu.VMEM((rows, width), F32)],
        compiler_params=_params("arbitrary"),
        name="s5_scan",
    )(u, bcat, ccat, ar, ai, d, glu_w.astype(BF16), glu_b.reshape(1, width))


def _mix_kernel(x_ref, ya_ref, yb_ref, yc_ref, wg_ref, wb_ref, wo_ref, g_ref, b_ref, o_ref, *, alpha):
    x = x_ref[...]
    xb = x.astype(BF16)
    d = x.shape[1]
    mix = None
    for k, y_ref in enumerate((ya_ref, yb_ref, yc_ref)):
        gate = jax.nn.sigmoid(_dot(xb, wg_ref[:, k * d:(k + 1) * d]))
        term = gate * _dot(y_ref[...], wb_ref[k])
        mix = term if mix is None else mix + term
    out = _dot(mix.astype(BF16), wo_ref[...])
    o_ref[...] = _layer_norm(alpha * x + out, g_ref[...], b_ref[...])


def _mix(x, ya, yb, yc_tm, w_gate, w_branch, w_out, g, b, alpha, seq):
    n, d = x.shape
    width = ya.shape[1]
    lt = seq // TM
    row = lambda wd_: pl.BlockSpec((TM, wd_), lambda i: (i, 0))
    return pl.pallas_call(
        functools.partial(_mix_kernel, alpha=alpha),
        grid=(n // TM,),
        in_specs=[row(d), row(width), row(width),
                  pl.BlockSpec((TM, width), lambda i: (i % lt, i // lt)),
                  _const_spec(w_gate.shape), _const_spec(w_branch.shape), _const_spec(w_out.shape),
                  _const_spec((1, d)), _const_spec((1, d))],
        out_specs=row(d),
        out_shape=jax.ShapeDtypeStruct((n, d), F32),
        compiler_params=_params("parallel"),
        name="branch_mix",
    )(x, ya, yb, yc_tm, w_gate, w_branch, w_out, g.reshape(1, d), b.reshape(1, d))


def _head_major_qk(w_qk):
    d, cols = w_qk.shape
    heads = cols // (4 * DIFF_QK_DIM)
    w = w_qk.reshape(d, 2, 2, heads, DIFF_QK_DIM)
    return w.transpose(0, 1, 3, 2, 4).reshape(d, cols)


def kernel(x, ffn1_gate, ffn1_up, ffn1_down, ln1_g, ln1_b, w_in, gmlp_ln_g, gmlp_ln_b, gmlp_ws, gmlp_bs, diff_lq1, diff_lk1, diff_lq2, diff_lk2, diff_subln_g, s5_lambda_re, s5_lambda_im, s5_log_step, s5_b_re, s5_b_im, s5_c_re, s5_c_im, s5_d, s5_glu_w, s5_glu_b, w_branch, w_out, ln2_g, ln2_b, ffn2_gate, ffn2_up, ffn2_down, ln3_g, ln3_b):
    bsz, seq, d = x.shape
    depth = w_in.shape[0]
    width = w_branch.shape[2]
    alpha = (2 * depth) ** 0.25
    n = bsz * seq
    gate0 = 4 * width + 2 * width
    xf = x.reshape(n, d)
    for i in range(depth):
        xf = _ffn(xf, ffn1_gate[i], ffn1_up[i], ffn1_down[i], ln1_g[i], ln1_b[i], alpha)

        w = w_in[i]
        w_front = jnp.concatenate(
            [w[:, :2 * width], _head_major_qk(w[:, 2 * width:4 * width]), w[:, 5 * width:gate0]],
            axis=1).astype(BF16)
        w_vt = w[:, 4 * width:5 * width].T.astype(BF16)
        bs_full = jnp.broadcast_to(gmlp_bs[i][:, :, None], (GMLP_GROUPS, CHUNK, width // GMLP_GROUPS))
        ya, q, k, vt, zs = _in_proj(xf, w_front, w_vt, gmlp_ln_g[i], gmlp_ln_b[i], gmlp_ws[i], bs_full,
                                    bsz, seq, width)

        lam_init = 0.8 - 0.6 * math.exp(-0.3 * i)
        lam = (jnp.exp(jnp.sum(diff_lq1[i].astype(F32) * diff_lk1[i].astype(F32)))
               - jnp.exp(jnp.sum(diff_lq2[i].astype(F32) * diff_lk2[i].astype(F32))) + lam_init)
        yb = _diff_attention(q, k, vt, lam,diff_subln_g[i], 1.0 - lam_init, bsz, seq)

        s5p = _s5_params(s5_lambda_re[i], s5_lambda_im[i], s5_log_step[i], s5_b_re[i], s5_b_im[i],
                         s5_c_re[i], s5_c_im[i], s5_d[i], bsz)
        yc = _s5(zs.reshape(seq * bsz, width), *s5p, s5_glu_w[i], s5_glu_b[i], bsz)

        xf = _mix(xf, ya, yb, yc.reshape(seq, bsz * width), w[:, gate0:].astype(BF16),
                  w_branch[i].astype(BF16), w_out[i].astype(BF16), ln2_g[i], ln2_b[i], alpha, seq)

        xf = _ffn(xf, ffn2_gate[i], ffn2_up[i], ffn2_down[i], ln3_g[i], ln3_b[i], alpha)
    return xf.reshape(bsz, seq, d)
```

```python
import functools
import math

import jax
import jax.numpy as jnp
from jax import lax
from jax.experimental import pallas as pl
from jax.experimental.pallas import tpu as pltpu

F32 = jnp.float32
BF16 = jnp.bfloat16

CHUNK = 128
GMLP_GROUPS = 4
DIFF_QK_DIM = 64
DIFF_V_DIM = 128
S5_GROUP_DIM = 16
S5_STATE = 64
N_BRANCH = 3
LN_EPS = 1e-5

LANES = 128
SUBLANES = 8
VMEM_LIMIT = 56 * 1024 * 1024

TM = 512
TF = 256
TQ = 512
S5_T = 128
S5_PITCH = S5_T + SUBLANES


def _layer_norm(y, g, b):
    mu = jnp.mean(y, axis=-1, keepdims=True)
    d = y - mu
    var = jnp.mean(d * d, axis=-1, keepdims=True)
    return d * lax.rsqrt(var + LN_EPS) * g + b


def _dot(a, b):
    return jnp.dot(a, b, preferred_element_type=F32)


def _const_spec(shape):
    zeros = (0,) * len(shape)
    return pl.BlockSpec(shape, lambda *_: zeros, pipeline_mode=pl.Buffered(1))


def _params(*sem):
    return pltpu.CompilerParams(dimension_semantics=sem, vmem_limit_bytes=VMEM_LIMIT)


def _ffn_kernel(x_ref, wg_ref, wu_ref, wd_ref, g_ref, b_ref, o_ref, h_ref, *, alpha):
    x = x_ref[...]
    xb = x.astype(BF16)
    d_ff = wg_ref.shape[1]
    for c in range(d_ff // TF):
        cols = slice(c * TF, (c + 1) * TF)
        gate = _dot(xb, wg_ref[:, cols])
        up = _dot(xb, wu_ref[:, cols])
        h_ref[:, cols] = (gate * jax.nn.sigmoid(gate) * up).astype(BF16)
    out = _dot(h_ref[...], wd_ref[...])
    o_ref[...] = _layer_norm(alpha * x + 0.5 * out, g_ref[...], b_ref[...])


def _ffn(x, wg, wu, wd, g, b, alpha):
    n, d = x.shape
    d_ff = wg.shape[1]
    row = pl.BlockSpec((TM, d), lambda i: (i, 0))
    return pl.pallas_call(
        functools.partial(_ffn_kernel, alpha=alpha),
        grid=(n // TM,),
        in_specs=[row, _const_spec((d, d_ff)), _const_spec((d, d_ff)), _const_spec((d_ff, d)),
                  _const_spec((1, d)), _const_spec((1, d))],
        out_specs=row,
        out_shape=jax.ShapeDtypeStruct((n, d), F32),
        scratch_shapes=[pltpu.VMEM((TM, d_ff), BF16)],
        compiler_params=_params("parallel"),
        name="ffn",
    )(x, wg.astype(BF16), wu.astype(BF16), wd.astype(BF16), g.reshape(1, d), b.reshape(1, d))


def _in_kernel(x_ref, w_ref, wvt_ref, lng_ref, lnb_ref, ws_ref, bs_ref,
               ya_ref, q_ref, k_ref, vt_ref, zs_ref, *, width, qk_scale):
    xb = x_ref[...].astype(BF16)
    tm = xb.shape[0]
    w2 = 2 * width

    zg = jax.nn.gelu(_dot(xb, w_ref[:, 0:w2]))
    u = zg[:, :width]
    vn = _layer_norm(zg[:, width:], lng_ref[...], lnb_ref[...]).astype(BF16)
    t_idx = lax.broadcasted_iota(jnp.int32, (CHUNK, CHUNK), 0)
    s_idx = lax.broadcasted_iota(jnp.int32, (CHUNK, CHUNK), 1)
    causal = s_idx <= t_idx
    gd = width // GMLP_GROUPS
    for g in range(GMLP_GROUPS):
        ws = jnp.where(causal, ws_ref[g], 0.0).astype(BF16)
        cols = slice(g * gd, (g + 1) * gd)
        for c in range(tm // CHUNK):
            rows = slice(c * CHUNK, (c + 1) * CHUNK)
            s = _dot(ws, vn[rows, cols]) + bs_ref[g]
            ya_ref[rows, cols] = (u[rows, cols] * s).astype(BF16)

    zqk = _dot(xb, w_ref[:, w2:2 * w2])
    q_ref[...] = (zqk[:, :width] * qk_scale).astype(BF16)
    k_ref[...] = zqk[:, width:].astype(BF16)
    nt = (((1,), (1,)), ((), ()))
    vt_ref[...] = lax.dot_general(wvt_ref[...], xb, nt, preferred_element_type=F32).astype(BF16)
    zs_ref[...] = _dot(xb, w_ref[:, 2 * w2:2 * w2 + width])


def _in_proj(x, w, wvt, lng, lnb, ws, bs, bsz, seq, width):
    n, d = x.shape
    lt = seq // TM
    row = lambda wd_: pl.BlockSpec((TM, wd_), lambda i: (i, 0))
    out_bf = jax.ShapeDtypeStruct((n, width), BF16)
    return pl.pallas_call(
        functools.partial(_in_kernel, width=width, qk_scale=math.log2(math.e) * DIFF_QK_DIM ** -0.5),
        grid=(n // TM,),
        in_specs=[row(d), _const_spec(w.shape), _const_spec(wvt.shape), _const_spec((1, width)),
                  _const_spec((1, width)), _const_spec(ws.shape), _const_spec(bs.shape)],
        out_specs=[row(width), row(width), row(width),
                   pl.BlockSpec((width, TM), lambda i: (i // lt, i % lt)),
                   row(width)],
        out_shape=[out_bf, out_bf, out_bf,
                   jax.ShapeDtypeStruct((bsz * width, seq), BF16),
                   jax.ShapeDtypeStruct((n, width), F32)],
        compiler_params=_params("parallel"),
        name="in_proj_gmlp",
    )(x, w, wvt, lng.reshape(1, width), lnb.reshape(1, width), ws, bs)


def _attn_kernel(lam_ref, q_ref, k_ref, vt_ref, g_ref, o_ref, acc1_ref, acc2_ref, *, out_scale):
    q = q_ref[...]
    lane = lax.broadcasted_iota(jnp.int32, q.shape, 1)
    zero = jnp.zeros_like(q)
    q1 = jnp.where(lane < DIFF_QK_DIM, q, zero)
    q2 = jnp.where(lane >= DIFF_QK_DIM, q, zero)
    nt = (((1,), (1,)), ((), ()))

    def softmax_step(s, m, l, mask):
        if mask is not None:
            s = jnp.where(mask, s, -jnp.inf)
        m_new = jnp.maximum(m, jnp.max(s, axis=0, keepdims=True))
        p = jnp.exp2(s - m_new)
        corr = jnp.exp2(m - m_new)
        l_new = corr * l + jnp.sum(p, axis=0, keepdims=True)
        return m_new, l_new, corr, p.astype(BF16)

    hq = TQ // 2
    halves = (slice(0, hq), slice(hq, TQ))
    streams = [(h, qm, acc) for h in (0, 1) for qm, acc in ((q1, acc1_ref), (q2, acc2_ref))]

    def scores(j, diagonal):
        out = []
        for h, qm, _ in streams:
            tk = hq if (diagonal and h == 0) else TQ
            out.append(lax.dot_general(k_ref[j * TQ:j * TQ + tk, :], qm[halves[h]], nt,
                                       preferred_element_type=F32))
        return out

    def consume(j, s, stats, diagonal):
        new, corr, p = [], [], []
        for n, (h, _, _) in enumerate(streams):
            m, l = stats[n]
            mask = (causal_lo if h == 0 else causal_hi) if diagonal else None
            m, l, c, pn = softmax_step(s[n], m, l, mask)
            new.append((m, l)); corr.append(c); p.append(pn)
        pv = [_dot(vt_ref[:, j * TQ:j * TQ + pn.shape[0]], pn) for pn in p]
        for n, (h, _, acc) in enumerate(streams):
            acc[:, halves[h]] = corr[n] * acc[:, halves[h]] + pv[n]
        return new

    iota = lambda rows, axis: lax.broadcasted_iota(jnp.int32, (rows, hq), axis)
    causal_lo = iota(hq, 0) <= iota(hq, 1)
    causal_hi = iota(TQ, 0) <= iota(TQ, 1) + hq

    def run(diag):
        acc1_ref[...] = jnp.zeros_like(acc1_ref)
        acc2_ref[...] = jnp.zeros_like(acc2_ref)
        stats = [(jnp.full((1, hq), -jnp.inf, F32), jnp.zeros((1, hq), F32))] * len(streams)
        s_next = scores(0, diag == 0)
        for j in range(diag + 1):
            s_cur = s_next
            if j < diag:
                s_next = scores(j + 1, j + 1 == diag)
            stats = consume(j, s_cur, stats, j == diag)
        for h in (0, 1):
            l1, l2 = stats[2 * h][1], stats[2 * h + 1][1]
            ot = acc1_ref[:, halves[h]] * (1.0 / l1) - acc2_ref[:, halves[h]] * (lam_ref[0] / l2)
            o = ot.T
            o = o * lax.rsqrt(jnp.mean(o * o, axis=-1, keepdims=True) + LN_EPS) * g_ref[...] * out_scale
            o_ref[halves[h], :] = o.astype(BF16)

    for diag in range(k_ref.shape[0] // TQ):
        pl.when(pl.program_id(2) == diag)(functools.partial(run, diag))


def _diff_attention(q, k, vt, lam, subln_g, out_scale, bsz, seq):
    n, width = q.shape
    heads = width // DIFF_V_DIM
    qt = seq // TQ
    q_spec = pl.BlockSpec((TQ, DIFF_V_DIM), lambda b, h, i: (b * qt + i, h))
    k_spec = pl.BlockSpec((seq, DIFF_V_DIM), lambda b, h, i: (b, h))
    vt_spec = pl.BlockSpec((DIFF_V_DIM, seq), lambda b, h, i: (b * heads + h, 0))
    return pl.pallas_call(
        functools.partial(_attn_kernel, out_scale=out_scale),
        grid=(bsz, heads, qt),
        in_specs=[pl.BlockSpec(memory_space=pltpu.SMEM), q_spec, k_spec, vt_spec,
                  pl.BlockSpec((1, DIFF_V_DIM), lambda b, h, i: (0, 0))],
        out_specs=q_spec,
        out_shape=jax.ShapeDtypeStruct((n, width), BF16),
        scratch_shapes=[pltpu.VMEM((DIFF_V_DIM, TQ), F32), pltpu.VMEM((DIFF_V_DIM, TQ), F32)],
        compiler_params=_params("parallel", "parallel", "arbitrary"),
        name="diff_attention",
    )(lam.reshape(1), q, k, vt, subln_g.reshape(1, DIFF_V_DIM))


def _s5_kernel(u_ref, bcat_ref, ccat_ref, ar_ref, ai_ref, d_ref, gw_ref, gb_ref,
               o_ref, h_ref, st_ref, y_ref):
    @pl.when(pl.program_id(0) == 0)
    def _():
        st_ref[...] = jnp.zeros_like(st_ref)

    bsz, steps, width = u_ref.shape
    u = u_ref[...].reshape(bsz * steps, width)
    ub = u.astype(BF16)
    n_blk = bcat_ref.shape[0]
    n_slab = h_ref.shape[0]
    half = n_slab // 2
    pitch = h_ref.shape[1] // bsz

    def slab_rows(k, b):
        return h_ref.at[k, b * pitch:b * pitch + steps, :]

    for j in range(n_blk):
        bu = _dot(ub[:, j * LANES:(j + 1) * LANES], bcat_ref[j])
        for k in range(n_slab):
            for b in range(bsz):
                slab_rows(k, b)[...] = bu[b * steps:(b + 1) * steps, k * LANES:(k + 1) * LANES]
        ar = [ar_ref[j, :, k * LANES:(k + 1) * LANES] for k in range(half)]
        ai = [ai_ref[j, :, k * LANES:(k + 1) * LANES] for k in range(half)]

        def step(t, carry):
            out = []
            for k in range(half):
                hr, hi = carry[2 * k], carry[2 * k + 1]
                at_t = pl.ds(t, bsz, stride=pitch)
                nhr = ar[k] * hr - ai[k] * hi + h_ref[k, at_t, :]
                nhi = ar[k] * hi + ai[k] * hr + h_ref[half + k, at_t, :]
                h_ref[k, at_t, :] = nhr
                h_ref[half + k, at_t, :] = nhi
                out += [nhr, nhi]
            return tuple(out)

        init = []
        for k in range(half):
            init += [st_ref[j, 0, :, k * LANES:(k + 1) * LANES], st_ref[j, 1, :, k * LANES:(k + 1) * LANES]]
        fin = lax.fori_loop(0, steps, step, tuple(init), unroll=8)
        for k in range(half):
            st_ref[j, 0, :, k * LANES:(k + 1) * LANES] = fin[2 * k]
            st_ref[j, 1, :, k * LANES:(k + 1) * LANES] = fin[2 * k + 1]
        h = jnp.concatenate(
            [jnp.concatenate([slab_rows(k, b)[...] for k in range(n_slab)], axis=1) for b in range(bsz)],
            axis=0)
        y_ref[:, j * LANES:(j + 1) * LANES] = _dot(h.astype(BF16), ccat_ref[j])

    y = jax.nn.gelu(y_ref[...] + d_ref[...] * u)
    gate = jax.nn.sigmoid(_dot(y.astype(BF16), gw_ref[...]) + gb_ref[...])
    o_ref[...] = (y * gate).astype(BF16).reshape(bsz, steps, width)


def _s5_params(lam_re, lam_im, log_step, b_re, b_im, c_re, c_im, d, bsz):
    g, p = lam_re.shape
    h = b_re.shape[-1]
    gl = LANES // h
    nb = g // gl
    lr = lam_re.astype(F32)
    li = lam_im.astype(F32)
    step = jnp.exp(log_step.astype(F32))[:, None]
    mag = jnp.exp(lr * step)
    ar = mag * jnp.cos(li * step)
    ai = mag * jnp.sin(li * step)
    den = lr * lr + li * li
    cr = ((ar - 1.0) * lr + ai * li) / den
    ci = (ai * lr - (ar - 1.0) * li) / den
    br = b_re.astype(F32)
    bi = b_im.astype(F32)
    bbar_re = cr[..., None] * br - ci[..., None] * bi
    bbar_im = cr[..., None] * bi + ci[..., None] * br
    eye = jnp.eye(gl, dtype=F32)

    def in_blocks(bb):
        t = bb.transpose(0, 2, 1).reshape(nb, gl, h, p)
        return jnp.einsum('jghp,gk->jghkp', t, eye).reshape(nb, gl * h, gl * p)

    def out_blocks(cc):
        t = cc.astype(F32).transpose(0, 2, 1).reshape(nb, gl, p, h)
        return jnp.einsum('jgph,gk->jgpkh', t, eye).reshape(nb, gl * p, gl * h)

    bcat = jnp.concatenate([in_blocks(bbar_re), in_blocks(bbar_im)], axis=-1).astype(BF16)
    ccat = jnp.concatenate([out_blocks(c_re), -out_blocks(c_im)], axis=1).astype(BF16)
    bro = lambda a: jnp.broadcast_to(a.reshape(nb, 1, gl * p), (nb, bsz, gl * p))
    return bcat, ccat, bro(ar), bro(ai), d.astype(F32).reshape(1, g * h)


def _s5(u, bcat, ccat, ar, ai, d, glu_w, glu_b):
    bsz, seq, width = u.shape
    nb, _, ns2 = bcat.shape
    blk = pl.BlockSpec((bsz, S5_T, width), lambda t: (0, t, 0))
    return pl.pallas_call(
        _s5_kernel,
        grid=(seq // S5_T,),
        in_specs=[blk, _const_spec(bcat.shape), _const_spec(ccat.shape), _const_spec(ar.shape),
                  _const_spec(ai.shape), _const_spec((1, width)), _const_spec((width, width)),
                  _const_spec((1, width))],
        out_specs=blk,
        out_shape=jax.ShapeDtypeStruct((bsz, seq, width), BF16),
        scratch_shapes=[pltpu.VMEM((ns2 // LANES, bsz * S5_PITCH, LANES), F32),
                        pltpu.VMEM((nb, 2, bsz, ns2 // 2), F32),
                        pltpu.VMEM((bsz * S5_T, width), F32)],
        compiler_params=_params("arbitrary"),
        name="s5_scan",
    )(u, bcat, ccat, ar, ai, d, glu_w.astype(BF16), glu_b.reshape(1, width))


def _mix_kernel(x_ref, ya_ref, yb_ref, yc_ref, wg_ref, wb_ref, wo_ref, g_ref, b_ref, o_ref, *, alpha):
    x = x_ref[...]
    xb = x.astype(BF16)
    d = x.shape[1]
    mix = None
    for k, y_ref in enumerate((ya_ref, yb_ref, yc_ref)):
        gate = jax.nn.sigmoid(_dot(xb, wg_ref[:, k * d:(k + 1) * d]))
        term = gate * _dot(y_ref[...], wb_ref[k])
        mix = term if mix is None else mix + term
    out = _dot(mix.astype(BF16), wo_ref[...])
    o_ref[...] = _layer_norm(alpha * x + out, g_ref[...], b_ref[...])


def _mix(x, ya, yb, yc, w_gate, w_branch, w_out, g, b, alpha):
    n, d = x.shape
    width = ya.shape[1]
    row = lambda wd_: pl.BlockSpec((TM, wd_), lambda i: (i, 0))
    return pl.pallas_call(
        functools.partial(_mix_kernel, alpha=alpha),
        grid=(n // TM,),
        in_specs=[row(d), row(width), row(width), row(width),
                  _const_spec(w_gate.shape), _const_spec(w_branch.shape), _const_spec(w_out.shape),
                  _const_spec((1, d)), _const_spec((1, d))],
        out_specs=row(d),
        out_shape=jax.ShapeDtypeStruct((n, d), F32),
        compiler_params=_params("parallel"),
        name="branch_mix",
    )(x, ya, yb, yc, w_gate, w_branch, w_out, g.reshape(1, d), b.reshape(1, d))


def _head_major_qk(w_qk):
    d, cols = w_qk.shape
    heads = cols // (4 * DIFF_QK_DIM)
    w = w_qk.reshape(d, 2, 2, heads, DIFF_QK_DIM)
    return w.transpose(0, 1, 3, 2, 4).reshape(d, cols)


def kernel(x, ffn1_gate, ffn1_up, ffn1_down, ln1_g, ln1_b, w_in, gmlp_ln_g, gmlp_ln_b, gmlp_ws, gmlp_bs, diff_lq1, diff_lk1, diff_lq2, diff_lk2, diff_subln_g, s5_lambda_re, s5_lambda_im, s5_log_step, s5_b_re, s5_b_im, s5_c_re, s5_c_im, s5_d, s5_glu_w, s5_glu_b, w_branch, w_out, ln2_g, ln2_b, ffn2_gate, ffn2_up, ffn2_down, ln3_g, ln3_b):
    bsz, seq, d = x.shape
    depth = w_in.shape[0]
    width = w_branch.shape[2]
    alpha = (2 * depth) ** 0.25
    n = bsz * seq
    gate0 = 4 * width + 2 * width
    xf = x.reshape(n, d)
    for i in range(depth):
        xf = _ffn(xf, ffn1_gate[i], ffn1_up[i], ffn1_down[i], ln1_g[i], ln1_b[i], alpha)

        w = w_in[i]
        w_front = jnp.concatenate(
            [w[:, :2 * width], _head_major_qk(w[:, 2 * width:4 * width]), w[:, 5 * width:gate0]],
            axis=1).astype(BF16)
        w_vt = w[:, 4 * width:5 * width].T.astype(BF16)
        bs_full = jnp.broadcast_to(gmlp_bs[i][:, :, None], (GMLP_GROUPS, CHUNK, width // GMLP_GROUPS))
        ya, q, k, vt, zs = _in_proj(xf, w_front, w_vt, gmlp_ln_g[i], gmlp_ln_b[i], gmlp_ws[i], bs_full,
                                    bsz, seq, width)

        lam_init = 0.8 - 0.6 * math.exp(-0.3 * i)
        lam = (jnp.exp(jnp.sum(diff_lq1[i].astype(F32) * diff_lk1[i].astype(F32)))
               - jnp.exp(jnp.sum(diff_lq2[i].astype(F32) * diff_lk2[i].astype(F32))) + lam_init)
        yb = _diff_attention(q, k, vt, lam, diff_subln_g[i], 1.0 - lam_init, bsz, seq)

        s5p = _s5_params(s5_lambda_re[i], s5_lambda_im[i], s5_log_step[i], s5_b_re[i], s5_b_im[i],
                         s5_c_re[i], s5_c_im[i], s5_d[i], bsz)
        yc = _s5(zs.reshape(bsz, seq, width), *s5p, s5_glu_w[i], s5_glu_b[i])

        xf = _mix(xf, ya, yb, yc.reshape(n, width), w[:, gate0:].astype(BF16),
                  w_branch[i].astype(BF16), w_out[i].astype(BF16), ln2_g[i], ln2_b[i], alpha)

        xf = _ffn(xf, ffn2_gate[i], ffn2_up[i], ffn2_down[i], ln3_g[i], ln3_b[i], alpha)
    return xf.reshape(bsz, seq, d)
```

```python
import functools
import math

import jax
import jax.numpy as jnp
from jax import lax
from jax.experimental import pallas as pl
from jax.experimental.pallas import tpu as pltpu

F32 = jnp.float32
BF16 = jnp.bfloat16

CHUNK = 128
GMLP_GROUPS = 4
DIFF_QK_DIM = 64
DIFF_V_DIM = 128
S5_GROUP_DIM = 16
S5_STATE = 64
N_BRANCH = 3
LN_EPS = 1e-5

LANES = 128
SUBLANES = 8
VMEM_LIMIT = 56 * 1024 * 1024

TM = 1024
TF = 256
TQ = 512
S5_T = 128
S5_PITCH = S5_T + SUBLANES


def _layer_norm(y, g, b):
    mu = jnp.mean(y, axis=-1, keepdims=True)
    d = y - mu
    var = jnp.mean(d * d, axis=-1, keepdims=True)
    return d * lax.rsqrt(var + LN_EPS) * g + b


def _dot(a, b):
    return jnp.dot(a, b, preferred_element_type=F32)


def _const_spec(shape):
    zeros = (0,) * len(shape)
    return pl.BlockSpec(shape, lambda *_: zeros, pipeline_mode=pl.Buffered(1))


def _layer_spec(shape, layer, col_block=0):
    idx = (layer,) + (0,) * (len(shape) - 1) + (col_block,)
    return pl.BlockSpec((None,) + tuple(shape), lambda *_: idx, pipeline_mode=pl.Buffered(1))


def _params(*sem):
    return pltpu.CompilerParams(dimension_semantics=sem, vmem_limit_bytes=VMEM_LIMIT)


def _ffn_kernel(x_ref, wg_ref, wu_ref, wd_ref, g_ref, b_ref, o_ref, h_ref, *, alpha):
    x = x_ref[...]
    xb = x.astype(BF16)
    d_ff = wg_ref.shape[1]
    for c in range(d_ff // TF):
        cols = slice(c * TF, (c + 1) * TF)
        gate = _dot(xb, wg_ref[:, cols])
        up = _dot(xb, wu_ref[:, cols])
        h_ref[:, cols] = (gate * jax.nn.sigmoid(gate) * up).astype(BF16)
    out = _dot(h_ref[...], wd_ref[...])
    o_ref[...] = _layer_norm(alpha * x + 0.5 * out, g_ref[...], b_ref[...])


def _ffn(x, wg, wu, wd, g, b, alpha, layer):
    n, d = x.shape
    d_ff = wg.shape[2]
    row = pl.BlockSpec((TM, d), lambda i: (i, 0))
    return pl.pallas_call(
        functools.partial(_ffn_kernel, alpha=alpha),
        grid=(n // TM,),
        in_specs=[row, _layer_spec((d, d_ff), layer), _layer_spec((d, d_ff), layer),
                  _layer_spec((d_ff, d), layer), _const_spec((1, d)), _const_spec((1, d))],
        out_specs=row,
        out_shape=jax.ShapeDtypeStruct((n, d), F32),
        scratch_shapes=[pltpu.VMEM((TM, d_ff), BF16)],
        compiler_params=_params("parallel"),
        name="ffn",
    )(x, wg, wu, wd, g.reshape(1, d), b.reshape(1, d))


def _in_kernel(x_ref, wgm_ref, wqk_ref, wvt_ref, ws5_ref, lng_ref, lnb_ref, ws_ref, bs_ref,
               ya_ref, q_ref, k_ref, vt_ref, zs_ref, *, width, qk_scale):
    xb = x_ref[...].astype(BF16)
    tm = xb.shape[0]

    zg = jax.nn.gelu(_dot(xb, wgm_ref[...]))
    u = zg[:, :width]
    vn = _layer_norm(zg[:, width:], lng_ref[...], lnb_ref[...]).astype(BF16)
    t_idx = lax.broadcasted_iota(jnp.int32, (CHUNK, CHUNK), 0)
    s_idx = lax.broadcasted_iota(jnp.int32, (CHUNK, CHUNK), 1)
    causal = s_idx <= t_idx
    gd = width // GMLP_GROUPS
    for g in range(GMLP_GROUPS):
        ws = jnp.where(causal, ws_ref[g], 0.0).astype(BF16)
        cols = slice(g * gd, (g + 1) * gd)
        for c in range(tm // CHUNK):
            rows = slice(c * CHUNK, (c + 1) * CHUNK)
            s = _dot(ws, vn[rows, cols]) + bs_ref[g]
            ya_ref[rows, cols] = (u[rows, cols] * s).astype(BF16)

    zqk = _dot(xb, wqk_ref[...])
    q_ref[...] = (zqk[:, :width] * qk_scale).astype(BF16)
    k_ref[...] = zqk[:, width:].astype(BF16)
    nt = (((1,), (1,)), ((), ()))
    vt_ref[...] = lax.dot_general(wvt_ref[...], xb, nt, preferred_element_type=F32).astype(BF16)
    zs_ref[...] = _dot(xb, ws5_ref[...])


def _in_proj(x, w_in, wqk, wvt, lng, lnb, ws, bs, bsz, seq, width, layer):
    n, d = x.shape
    lt = seq // TM
    row = lambda wd_: pl.BlockSpec((TM, wd_), lambda i: (i, 0))
    out_bf = jax.ShapeDtypeStruct((n, width), BF16)
    return pl.pallas_call(
        functools.partial(_in_kernel, width=width, qk_scale=math.log2(math.e) * DIFF_QK_DIM ** -0.5),
        grid=(n // TM,),
        in_specs=[row(d), _layer_spec((d, 2 * width), layer), _const_spec(wqk.shape), _const_spec(wvt.shape),
                  _layer_spec((d, width), layer, col_block=5),
                  _const_spec((1, width)), _const_spec((1, width)), _const_spec(ws.shape),
                  _const_spec(bs.shape)],
        out_specs=[row(width), row(width), row(width),
                   pl.BlockSpec((width, TM), lambda i: (i // lt, i % lt)),
                   row(width)],
        out_shape=[out_bf, out_bf, out_bf,
                   jax.ShapeDtypeStruct((bsz * width, seq), BF16),
                   jax.ShapeDtypeStruct((n, width), F32)],
        compiler_params=_params("parallel"),
        name="in_proj_gmlp",
    )(x, w_in, wqk, wvt, w_in, lng.reshape(1, width), lnb.reshape(1, width), ws, bs)


def _attn_kernel(lam_ref, q_ref, k_ref, vt_ref, g_ref, o_ref, acc1_ref, acc2_ref, *, out_scale):
    q = q_ref[...]
    lane = lax.broadcasted_iota(jnp.int32, q.shape, 1)
    zero = jnp.zeros_like(q)
    q1 = jnp.where(lane < DIFF_QK_DIM, q, zero)
    q2 = jnp.where(lane >= DIFF_QK_DIM, q, zero)
    nt = (((1,), (1,)), ((), ()))

    def softmax_step(s, m, l, mask):
        if mask is not None:
            s = jnp.where(mask, s, -jnp.inf)
        m_new = jnp.maximum(m, jnp.max(s, axis=0, keepdims=True))
        p = jnp.exp2(s - m_new)
        corr = jnp.exp2(m - m_new)
        l_new = corr * l + jnp.sum(p, axis=0, keepdims=True)
        return m_new, l_new, corr, p.astype(BF16)

    hq = TQ // 2
    halves = (slice(0, hq), slice(hq, TQ))
    streams = [(h, qm, acc) for h in (0, 1) for qm, acc in ((q1, acc1_ref), (q2, acc2_ref))]

    def scores(j, diagonal):
        out = []
        for h, qm, _ in streams:
            tk = hq if (diagonal and h == 0) else TQ
            out.append(lax.dot_general(k_ref[j * TQ:j * TQ + tk, :], qm[halves[h]], nt,
                                       preferred_element_type=F32))
        return out

    def consume(j, s, stats, diagonal):
        new, corr, p = [], [], []
        for n, (h, _, _) in enumerate(streams):
            m, l = stats[n]
            mask = (causal_lo if h == 0 else causal_hi) if diagonal else None
            m, l, c, pn = softmax_step(s[n], m, l, mask)
            new.append((m, l)); corr.append(c); p.append(pn)
        pv = [_dot(vt_ref[:, j * TQ:j * TQ + pn.shape[0]], pn) for pn in p]
        for n, (h, _, acc) in enumerate(streams):
            acc[:, halves[h]] = corr[n] * acc[:, halves[h]] + pv[n]
        return new

    iota = lambda rows, axis: lax.broadcasted_iota(jnp.int32, (rows, hq), axis)
    causal_lo = iota(hq, 0) <= iota(hq, 1)
    causal_hi = iota(TQ, 0) <= iota(TQ, 1) + hq

    def run(diag):
        acc1_ref[...] = jnp.zeros_like(acc1_ref)
        acc2_ref[...] = jnp.zeros_like(acc2_ref)
        stats = [(jnp.full((1, hq), -jnp.inf, F32), jnp.zeros((1, hq), F32))] * len(streams)
        s_next = scores(0, diag == 0)
        for j in range(diag + 1):
            s_cur = s_next
            if j < diag:
                s_next = scores(j + 1, j + 1 == diag)
            stats = consume(j, s_cur, stats, j == diag)
        for h in (0, 1):
            l1, l2 = stats[2 * h][1], stats[2 * h + 1][1]
            ot = acc1_ref[:, halves[h]] * (1.0 / l1) - acc2_ref[:, halves[h]] * (lam_ref[0] / l2)
            o = ot.T
            o = o * lax.rsqrt(jnp.mean(o * o, axis=-1, keepdims=True) + LN_EPS) * g_ref[...] * out_scale
            o_ref[halves[h], :] = o.astype(BF16)

    for diag in range(k_ref.shape[0] // TQ):
        pl.when(pl.program_id(2) == diag)(functools.partial(run, diag))


def _diff_attention(q, k, vt, lam, subln_g, out_scale, bsz, seq):
    n, width = q.shape
    heads = width // DIFF_V_DIM
    qt = seq // TQ
    q_spec = pl.BlockSpec((TQ, DIFF_V_DIM), lambda b, h, i: (b * qt + i, h))
    k_spec = pl.BlockSpec((seq, DIFF_V_DIM), lambda b, h, i: (b, h))
    vt_spec = pl.BlockSpec((DIFF_V_DIM, seq), lambda b, h, i: (b * heads + h, 0))
    return pl.pallas_call(
        functools.partial(_attn_kernel, out_scale=out_scale),
        grid=(bsz, heads, qt),
        in_specs=[pl.BlockSpec(memory_space=pltpu.SMEM), q_spec, k_spec, vt_spec,
                  pl.BlockSpec((1, DIFF_V_DIM), lambda b, h, i: (0, 0))],
        out_specs=q_spec,
        out_shape=jax.ShapeDtypeStruct((n, width), BF16),
        scratch_shapes=[pltpu.VMEM((DIFF_V_DIM, TQ), F32), pltpu.VMEM((DIFF_V_DIM, TQ), F32)],
        compiler_params=_params("parallel", "parallel", "arbitrary"),
        name="diff_attention",
    )(lam.reshape(1), q, k, vt, subln_g.reshape(1, DIFF_V_DIM))


def _s5_kernel(u_ref, bcat_ref, ccat_ref, ar_ref, ai_ref, d_ref, gw_ref, gb_ref,
               o_ref, h_ref, st_ref, y_ref):
    @pl.when(pl.program_id(0) == 0)
    def _():
        st_ref[...] = jnp.zeros_like(st_ref)

    bsz, steps, width = u_ref.shape
    u = u_ref[...].reshape(bsz * steps, width)
    ub = u.astype(BF16)
    n_blk = bcat_ref.shape[0]
    n_slab = h_ref.shape[0]
    half = n_slab // 2
    pitch = h_ref.shape[1] // bsz

    def slab_rows(k, b):
        return h_ref.at[k, b * pitch:b * pitch + steps, :]

    for j in range(n_blk):
        bu = _dot(ub[:, j * LANES:(j + 1) * LANES], bcat_ref[j])
        for k in range(n_slab):
            for b in range(bsz):
                slab_rows(k, b)[...] = bu[b * steps:(b + 1) * steps, k * LANES:(k + 1) * LANES]
        ar = [ar_ref[j, :, k * LANES:(k + 1) * LANES] for k in range(half)]
        ai = [ai_ref[j, :, k * LANES:(k + 1) * LANES] for k in range(half)]

        def step(t, carry):
            out = []
            for k in range(half):
                hr, hi = carry[2 * k], carry[2 * k + 1]
                at_t = pl.ds(t, bsz, stride=pitch)
                nhr = ar[k] * hr - ai[k] * hi + h_ref[k, at_t, :]
                nhi = ar[k] * hi + ai[k] * hr + h_ref[half + k, at_t, :]
                h_ref[k, at_t, :] = nhr
                h_ref[half + k, at_t, :] = nhi
                out += [nhr, nhi]
            return tuple(out)

        init = []
        for k in range(half):
            init += [st_ref[j, 0, :, k * LANES:(k + 1) * LANES], st_ref[j, 1, :, k * LANES:(k + 1) * LANES]]
        fin = lax.fori_loop(0, steps, step, tuple(init), unroll=8)
        for k in range(half):
            st_ref[j, 0, :, k * LANES:(k + 1) * LANES] = fin[2 * k]
            st_ref[j, 1, :, k * LANES:(k + 1) * LANES] = fin[2 * k + 1]
        h = jnp.concatenate(
            [jnp.concatenate([slab_rows(k, b)[...] for k in range(n_slab)], axis=1) for b in range(bsz)],
            axis=0)
        y_ref[:, j * LANES:(j + 1) * LANES] = _dot(h.astype(BF16), ccat_ref[j])

    y = jax.nn.gelu(y_ref[...] + d_ref[...] * u)
    gate = jax.nn.sigmoid(_dot(y.astype(BF16), gw_ref[...]) + gb_ref[...])
    o_ref[...] = (y * gate).astype(BF16).reshape(bsz, steps, width)


def _s5_params(lam_re, lam_im, log_step, b_re, b_im, c_re, c_im, d, bsz):
    g, p = lam_re.shape
    h = b_re.shape[-1]
    gl = LANES // h
    nb = g // gl
    lr = lam_re.astype(F32)
    li = lam_im.astype(F32)
    step = jnp.exp(log_step.astype(F32))[:, None]
    mag = jnp.exp(lr * step)
    ar = mag * jnp.cos(li * step)
    ai = mag * jnp.sin(li * step)
    den = lr * lr + li * li
    cr = ((ar - 1.0) * lr + ai * li) / den
    ci = (ai * lr - (ar - 1.0) * li) / den
    br = b_re.astype(F32)
    bi = b_im.astype(F32)
    bbar_re = cr[..., None] * br - ci[..., None] * bi
    bbar_im = cr[..., None] * bi + ci[..., None] * br
    eye = jnp.eye(gl, dtype=F32)

    def in_blocks(bb):
        t = bb.transpose(0, 2, 1).reshape(nb, gl, h, p)
        return jnp.einsum('jghp,gk->jghkp', t, eye).reshape(nb, gl * h, gl * p)

    def out_blocks(cc):
        t = cc.astype(F32).transpose(0, 2, 1).reshape(nb, gl, p, h)
        return jnp.einsum('jgph,gk->jgpkh', t, eye).reshape(nb, gl * p, gl * h)

    bcat = jnp.concatenate([in_blocks(bbar_re), in_blocks(bbar_im)], axis=-1).astype(BF16)
    ccat = jnp.concatenate([out_blocks(c_re), -out_blocks(c_im)], axis=1).astype(BF16)
    bro = lambda a: jnp.broadcast_to(a.reshape(nb, 1, gl * p), (nb, bsz, gl * p))
    return bcat, ccat, bro(ar), bro(ai), d.astype(F32).reshape(1, g * h)


def _s5(u, bcat, ccat, ar, ai, d, glu_w, glu_b):
    bsz, seq, width = u.shape
    nb, _, ns2 = bcat.shape
    blk = pl.BlockSpec((bsz, S5_T, width), lambda t: (0, t, 0))
    return pl.pallas_call(
        _s5_kernel,
        grid=(seq // S5_T,),
        in_specs=[blk, _const_spec(bcat.shape), _const_spec(ccat.shape), _const_spec(ar.shape),
                  _const_spec(ai.shape), _const_spec((1, width)), _const_spec((width, width)),
                  _const_spec((1, width))],
        out_specs=blk,
        out_shape=jax.ShapeDtypeStruct((bsz, seq, width), BF16),
        scratch_shapes=[pltpu.VMEM((ns2 // LANES, bsz * S5_PITCH, LANES), F32),
                        pltpu.VMEM((nb, 2, bsz, ns2 // 2), F32),
                        pltpu.VMEM((bsz * S5_T, width), F32)],
        compiler_params=_params("arbitrary"),
        name="s5_scan",
    )(u, bcat, ccat, ar, ai, d, glu_w.astype(BF16), glu_b.reshape(1, width))


def _mix_kernel(x_ref, ya_ref, yb_ref, yc_ref, wg_ref, wb_ref, wo_ref, g_ref, b_ref, o_ref, *, alpha):
    x = x_ref[...]
    xb = x.astype(BF16)
    d = x.shape[1]
    mix = None
    for k, y_ref in enumerate((ya_ref, yb_ref, yc_ref)):
        gate = jax.nn.sigmoid(_dot(xb, wg_ref[:, k * d:(k + 1) * d]))
        term = gate * _dot(y_ref[...], wb_ref[k])
        mix = term if mix is None else mix + term
    out = _dot(mix.astype(BF16), wo_ref[...])
    o_ref[...] = _layer_norm(alpha * x + out, g_ref[...], b_ref[...])


def _mix(x, ya, yb, yc, w_in, w_branch, w_out, g, b, alpha, layer):
    n, d = x.shape
    width = ya.shape[1]
    row = lambda wd_: pl.BlockSpec((TM, wd_), lambda i: (i, 0))
    assert w_in.shape[2] == 2 * N_BRANCH * d
    return pl.pallas_call(
        functools.partial(_mix_kernel, alpha=alpha),
        grid=(n // TM,),
        in_specs=[row(d), row(width), row(width), row(width),
                  _layer_spec((d, N_BRANCH * d), layer, col_block=1),
                  _layer_spec(w_branch.shape[1:], layer), _layer_spec((d, d), layer),
                  _const_spec((1, d)), _const_spec((1, d))],
        out_specs=row(d),
        out_shape=jax.ShapeDtypeStruct((n, d), F32),
        compiler_params=_params("parallel"),
        name="branch_mix",
    )(x, ya, yb, yc, w_in, w_branch, w_out, g.reshape(1, d), b.reshape(1, d))


def _head_major_qk(w_qk):
    d, cols = w_qk.shape
    heads = cols // (4 * DIFF_QK_DIM)
    w = w_qk.reshape(d, 2, 2, heads, DIFF_QK_DIM)
    return w.transpose(0, 1, 3, 2, 4).reshape(d, cols)


def kernel(x, ffn1_gate, ffn1_up, ffn1_down, ln1_g, ln1_b, w_in, gmlp_ln_g, gmlp_ln_b, gmlp_ws, gmlp_bs, diff_lq1, diff_lk1, diff_lq2, diff_lk2, diff_subln_g, s5_lambda_re, s5_lambda_im, s5_log_step, s5_b_re, s5_b_im, s5_c_re, s5_c_im, s5_d, s5_glu_w, s5_glu_b, w_branch, w_out, ln2_g, ln2_b, ffn2_gate, ffn2_up, ffn2_down, ln3_g, ln3_b):
    bsz, seq, d = x.shape
    depth = w_in.shape[0]
    width = w_branch.shape[2]
    alpha = (2 * depth) ** 0.25
    n = bsz * seq
    xf = x.reshape(n, d)
    cast = lambda a: a.astype(BF16)
    ffn1 = (cast(ffn1_gate), cast(ffn1_up), cast(ffn1_down))
    ffn2 = (cast(ffn2_gate), cast(ffn2_up), cast(ffn2_down))
    w_in_b, w_branch_b, w_out_b = cast(w_in), cast(w_branch), cast(w_out)
    for i in range(depth):
        xf = _ffn(xf, *ffn1, ln1_g[i], ln1_b[i], alpha, i)

        w_qk = _head_major_qk(w_in_b[i, :, 2 * width:4 * width])
        w_vt = w_in_b[i, :, 4 * width:5 * width].T
        bs_full = jnp.broadcast_to(gmlp_bs[i][:, :, None], (GMLP_GROUPS, CHUNK, width // GMLP_GROUPS))
        ya, q, k, vt, zs = _in_proj(xf, w_in_b, w_qk, w_vt, gmlp_ln_g[i], gmlp_ln_b[i], gmlp_ws[i], bs_full,
                                    bsz, seq, width, i)

        lam_init = 0.8 - 0.6 * math.exp(-0.3 * i)
        lam = (jnp.exp(jnp.sum(diff_lq1[i].astype(F32) * diff_lk1[i].astype(F32)))
               - jnp.exp(jnp.sum(diff_lq2[i].astype(F32) * diff_lk2[i].astype(F32))) + lam_init)
        yb = _diff_attention(q, k, vt, lam, diff_subln_g[i], 1.0 - lam_init, bsz, seq)

        s5p = _s5_params(s5_lambda_re[i], s5_lambda_im[i], s5_log_step[i], s5_b_re[i], s5_b_im[i],
                         s5_c_re[i], s5_c_im[i], s5_d[i], bsz)
        yc = _s5(zs.reshape(bsz, seq, width), *s5p, s5_glu_w[i], s5_glu_b[i])

        xf = _mix(xf, ya, yb, yc.reshape(n, width), w_in_b, w_branch_b, w_out_b,
                  ln2_g[i], ln2_b[i], alpha, i)

        xf = _ffn(xf, *ffn2, ln3_g[i], ln3_b[i], alpha, i)
    return xf.reshape(bsz, seq, d)
```

```python
import functools
import math

import jax
import jax.numpy as jnp
from jax import lax
from jax.experimental import pallas as pl
from jax.experimental.pallas import tpu as pltpu

F32 = jnp.float32
BF16 = jnp.bfloat16

CHUNK = 128
GMLP_GROUPS = 4
DIFF_QK_DIM = 64
DIFF_V_DIM = 128
S5_GROUP_DIM = 16
S5_STATE = 64
N_BRANCH = 3
LN_EPS = 1e-5

LANES = 128
SUBLANES = 8
VMEM_LIMIT = 56 * 1024 * 1024

TM = 1024
TF = 256
TQ = 512
S5_T = 128
S5_PITCH = S5_T + SUBLANES


def _layer_norm(y, g, b):
    mu = jnp.mean(y, axis=-1, keepdims=True)
    d = y - mu
    var = jnp.mean(d * d, axis=-1, keepdims=True)
    return d * lax.rsqrt(var + LN_EPS) * g + b


def _dot(a, b):
    return jnp.dot(a, b, preferred_element_type=F32)


def _const_spec(shape):
    zeros = (0,) * len(shape)
    return pl.BlockSpec(shape, lambda *_: zeros, pipeline_mode=pl.Buffered(1))


def _layer_spec(shape, layer, col_block=0):
    idx = (layer,) + (0,) * (len(shape) - 1) + (col_block,)
    return pl.BlockSpec((None,) + tuple(shape), lambda *_: idx, pipeline_mode=pl.Buffered(1))


def _params(*sem):
    return pltpu.CompilerParams(dimension_semantics=sem, vmem_limit_bytes=VMEM_LIMIT)


def _ffn_kernel(x_ref, wg_ref, wu_ref, wd_ref, g_ref, b_ref, o_ref, h_ref, *, alpha):
    x = x_ref[...]
    xb = x.astype(BF16)
    d_ff = wg_ref.shape[1]
    for c in range(d_ff // TF):
        cols = slice(c * TF, (c + 1) * TF)
        gate = _dot(xb, wg_ref[:, cols])
        up = _dot(xb, wu_ref[:, cols])
        h_ref[:, cols] = (gate * jax.nn.sigmoid(gate) * up).astype(BF16)
    out = _dot(h_ref[...], wd_ref[...])
    o_ref[...] = _layer_norm(alpha * x + 0.5 * out, g_ref[...], b_ref[...])


def _ffn(x, wg, wu, wd, g, b, alpha, layer):
    n, d = x.shape
    d_ff = wg.shape[2]
    row = pl.BlockSpec((TM, d), lambda i: (i, 0))
    return pl.pallas_call(
        functools.partial(_ffn_kernel, alpha=alpha),
        grid=(n // TM,),
        in_specs=[row, _layer_spec((d, d_ff), layer), _layer_spec((d, d_ff), layer),
                  _layer_spec((d_ff, d), layer), _const_spec((1, d)), _const_spec((1, d))],
        out_specs=row,
        out_shape=jax.ShapeDtypeStruct((n, d), F32),
        scratch_shapes=[pltpu.VMEM((TM, d_ff), BF16)],
        compiler_params=_params("parallel"),
        name="ffn",
    )(x, wg, wu, wd, g.reshape(1, d), b.reshape(1, d))


def _in_kernel(x_ref, wgm_ref, wqk_ref, wvt_ref, ws5_ref, lng_ref, lnb_ref, ws_ref, bs_ref,
               ya_ref, q_ref, k_ref, vt_ref, zs_ref, *, width, qk_scale):
    xb = x_ref[...].astype(BF16)
    tm = xb.shape[0]

    zg = jax.nn.gelu(_dot(xb, wgm_ref[...]))
    u = zg[:, :width]
    vn = _layer_norm(zg[:, width:], lng_ref[...], lnb_ref[...]).astype(BF16)
    t_idx = lax.broadcasted_iota(jnp.int32, (CHUNK, CHUNK), 0)
    s_idx = lax.broadcasted_iota(jnp.int32, (CHUNK, CHUNK), 1)
    causal = s_idx <= t_idx
    gd = width // GMLP_GROUPS
    for g in range(GMLP_GROUPS):
        ws = jnp.where(causal, ws_ref[g], 0.0).astype(BF16)
        cols = slice(g * gd, (g + 1) * gd)
        for c in range(tm // CHUNK):
            rows = slice(c * CHUNK, (c + 1) * CHUNK)
            s = _dot(ws, vn[rows, cols]) + bs_ref[g]
            ya_ref[rows, cols] = (u[rows, cols] * s).astype(BF16)

    zqk = _dot(xb, wqk_ref[...])
    q_ref[...] = (zqk[:, :width] * qk_scale).astype(BF16)
    k_ref[...] = zqk[:, width:].astype(BF16)
    nt = (((1,), (1,)), ((), ()))
    vt_ref[...] = lax.dot_general(wvt_ref[...], xb, nt, preferred_element_type=F32).astype(BF16)
    zs_ref[...] = _dot(xb, ws5_ref[...])


def _in_proj(x, w_in, wqk, wvt, lng, lnb, ws, bs, bsz, seq, width, layer):
    n, d = x.shape
    lt = seq // TM
    row = lambda wd_: pl.BlockSpec((TM, wd_), lambda i: (i, 0))
    out_bf = jax.ShapeDtypeStruct((n, width), BF16)
    return pl.pallas_call(
        functools.partial(_in_kernel, width=width, qk_scale=math.log2(math.e) * DIFF_QK_DIM ** -0.5),
        grid=(n // TM,),
        in_specs=[row(d), _layer_spec((d, 2 * width), layer), _const_spec(wqk.shape), _const_spec(wvt.shape),
                  _layer_spec((d, width), layer, col_block=5),
                  _const_spec((1, width)), _const_spec((1, width)), _const_spec(ws.shape),
                  _const_spec(bs.shape)],
        out_specs=[row(width), row(width), row(width),
                   pl.BlockSpec((width, TM), lambda i: (i // lt, i % lt)),
                   row(width)],
        out_shape=[out_bf, out_bf, out_bf,
                   jax.ShapeDtypeStruct((bsz * width, seq), BF16),
                   jax.ShapeDtypeStruct((n, width), F32)],
        compiler_params=_params("parallel"),
        name="in_proj_gmlp",
    )(x, w_in, wqk, wvt, w_in, lng.reshape(1, width), lnb.reshape(1, width), ws, bs)


def _attn_kernel(lam_ref, q_ref, k_ref, vt_ref, g_ref, o_ref, acc1_ref, acc2_ref, *, out_scale):
    q = q_ref[...]
    lane = lax.broadcasted_iota(jnp.int32, q.shape, 1)
    zero = jnp.zeros_like(q)
    q1 = jnp.where(lane < DIFF_QK_DIM, q, zero)
    q2 = jnp.where(lane >= DIFF_QK_DIM, q, zero)
    nt = (((1,), (1,)), ((), ()))

    def softmax_step(s, m, l, mask):
        if mask is not None:
            s = jnp.where(mask, s, -jnp.inf)
        m_new = jnp.maximum(m, jnp.max(s, axis=0, keepdims=True))
        p = jnp.exp2(s - m_new)
        corr = jnp.exp2(m - m_new)
        l_new = corr * l + jnp.sum(p, axis=0, keepdims=True)
        return m_new, l_new, corr, p.astype(BF16)

    hq = TQ // 2
    halves = (slice(0, hq), slice(hq, TQ))
    streams = [(h, qm, acc) for h in (0, 1) for qm, acc in ((q1, acc1_ref), (q2, acc2_ref))]

    def scores(tile, j):
        out = []
        for h, qm, _ in streams:
            tk = hq if (j == tile and h == 0) else TQ
            rows = slice(tile * TQ + h * hq, tile * TQ + (h + 1) * hq)
            out.append(lax.dot_general(k_ref[j * TQ:j * TQ + tk, :], qm[rows], nt,
                                       preferred_element_type=F32))
        return out

    def consume(j, s, stats, diagonal):
        new, corr, p = [], [], []
        for n, (h, _, _) in enumerate(streams):
            m, l = stats[n]
            mask = (causal_lo if h == 0 else causal_hi) if diagonal else None
            m, l, c, pn = softmax_step(s[n], m, l, mask)
            new.append((m, l)); corr.append(c); p.append(pn)
        pv = [_dot(vt_ref[:, j * TQ:j * TQ + pn.shape[0]], pn) for pn in p]
        for n, (h, _, acc) in enumerate(streams):
            acc[:, halves[h]] = corr[n] * acc[:, halves[h]] + pv[n]
        return new

    iota = lambda rows, axis: lax.broadcasted_iota(jnp.int32, (rows, hq), axis)
    causal_lo = iota(hq, 0) <= iota(hq, 1)
    causal_hi = iota(TQ, 0) <= iota(TQ, 1) + hq

    def run(diag):
        acc1_ref[...] = jnp.zeros_like(acc1_ref)
        acc2_ref[...] = jnp.zeros_like(acc2_ref)
        stats = [(jnp.full((1, hq), -jnp.inf, F32), jnp.zeros((1, hq), F32))] * len(streams)
        s_next = scores(diag, 0)
        for j in range(diag + 1):
            s_cur = s_next
            if j < diag:
                s_next = scores(diag, j + 1)
            stats = consume(j, s_cur, stats, j == diag)
        for h in (0, 1):
            l1, l2 = stats[2 * h][1], stats[2 * h + 1][1]
            ot = acc1_ref[:, halves[h]] * (1.0 / l1) - acc2_ref[:, halves[h]] * (lam_ref[0] / l2)
            o = ot.T
            o = o * lax.rsqrt(jnp.mean(o * o, axis=-1, keepdims=True) + LN_EPS) * g_ref[...] * out_scale
            o_ref[diag * TQ + h * hq:diag * TQ + (h + 1) * hq, :] = o.astype(BF16)

    for diag in range(k_ref.shape[0] // TQ):
        run(diag)


def _diff_attention(q, k, vt, lam, subln_g, out_scale, bsz, seq):
    n, width = q.shape
    heads = width // DIFF_V_DIM
    qk_spec = pl.BlockSpec((seq, DIFF_V_DIM), lambda b, h: (b, h))
    vt_spec = pl.BlockSpec((DIFF_V_DIM, seq), lambda b, h: (b * heads + h, 0))
    return pl.pallas_call(
        functools.partial(_attn_kernel, out_scale=out_scale),
        grid=(bsz, heads),
        in_specs=[pl.BlockSpec(memory_space=pltpu.SMEM), qk_spec, qk_spec, vt_spec,
                  pl.BlockSpec((1, DIFF_V_DIM), lambda b, h: (0, 0))],
        out_specs=qk_spec,
        out_shape=jax.ShapeDtypeStruct((n, width), BF16),
        scratch_shapes=[pltpu.VMEM((DIFF_V_DIM, TQ), F32), pltpu.VMEM((DIFF_V_DIM, TQ), F32)],
        compiler_params=_params("parallel", "parallel"),
        name="diff_attention",
    )(lam.reshape(1), q, k, vt, subln_g.reshape(1, DIFF_V_DIM))


def _s5_kernel(u_ref, bcat_ref, ccat_ref, ar_ref, ai_ref, d_ref, gw_ref, gb_ref,
               o_ref, h_ref, st_ref, y_ref):
    @pl.when(pl.program_id(0) == 0)
    def _():
        st_ref[...] = jnp.zeros_like(st_ref)

    bsz, steps, width = u_ref.shape
    u = u_ref[...].reshape(bsz * steps, width)
    ub = u.astype(BF16)
    n_blk = bcat_ref.shape[0]
    n_slab = h_ref.shape[1]
    half = n_slab // 2
    pitch = h_ref.shape[2] // bsz

    def slab_rows(buf, k, b):
        return h_ref.at[buf, k, b * pitch:b * pitch + steps, :]

    def project_in(j):
        bu = _dot(ub[:, j * LANES:(j + 1) * LANES], bcat_ref[j])
        for k in range(n_slab):
            for b in range(bsz):
                slab_rows(j % 2, k, b)[...] = bu[b * steps:(b + 1) * steps, k * LANES:(k + 1) * LANES]

    project_in(0)
    for j in range(n_blk):
        if j + 1 < n_blk:
            project_in(j + 1)
        buf = j % 2
        ar = [ar_ref[j, :, k * LANES:(k + 1) * LANES] for k in range(half)]
        ai = [ai_ref[j, :, k * LANES:(k + 1) * LANES] for k in range(half)]
        hr = [st_ref[j, 0, :, k * LANES:(k + 1) * LANES] for k in range(half)]
        hi = [st_ref[j, 1, :, k * LANES:(k + 1) * LANES] for k in range(half)]
        for t in range(steps):
            at_t = pl.ds(t, bsz, stride=pitch)
            for k in range(half):
                nhr = ar[k] * hr[k] - ai[k] * hi[k] + h_ref[buf, k, at_t, :]
                nhi = ar[k] * hi[k] + ai[k] * hr[k] + h_ref[buf, half + k, at_t, :]
                h_ref[buf, k, at_t, :] = nhr
                h_ref[buf, half + k, at_t, :] = nhi
                hr[k], hi[k] = nhr, nhi
        for k in range(half):
            st_ref[j, 0, :, k * LANES:(k + 1) * LANES] = hr[k]
            st_ref[j, 1, :, k * LANES:(k + 1) * LANES] = hi[k]
        h = jnp.concatenate(
            [jnp.concatenate([slab_rows(buf, k, b)[...] for k in range(n_slab)], axis=1) for b in range(bsz)],
            axis=0)
        y_ref[:, j * LANES:(j + 1) * LANES] = _dot(h.astype(BF16), ccat_ref[j])

    y = jax.nn.gelu(y_ref[...] + d_ref[...] * u)
    gate = jax.nn.sigmoid(_dot(y.astype(BF16), gw_ref[...]) + gb_ref[...])
    o_ref[...] = (y * gate).astype(BF16).reshape(bsz, steps, width)


def _s5_params(lam_re, lam_im, log_step, b_re, b_im, c_re, c_im, d, bsz):
    g, p = lam_re.shape
    h = b_re.shape[-1]
    gl = LANES // h
    nb = g // gl
    lr = lam_re.astype(F32)
    li = lam_im.astype(F32)
    step = jnp.exp(log_step.astype(F32))[:, None]
    mag = jnp.exp(lr * step)
    ar = mag * jnp.cos(li * step)
    ai = mag * jnp.sin(li * step)
    den = lr * lr + li * li
    cr = ((ar - 1.0) * lr + ai * li) / den
    ci = (ai * lr - (ar - 1.0) * li) / den
    br = b_re.astype(F32)
    bi = b_im.astype(F32)
    bbar_re = cr[..., None] * br - ci[..., None] * bi
    bbar_im = cr[..., None] * bi + ci[..., None] * br
    eye = jnp.eye(gl, dtype=F32)

    def in_blocks(bb):
        t = bb.transpose(0, 2, 1).reshape(nb, gl, h, p)
        return jnp.einsum('jghp,gk->jghkp', t, eye).reshape(nb, gl * h, gl * p)

    def out_blocks(cc):
        t = cc.astype(F32).transpose(0, 2, 1).reshape(nb, gl, p, h)
        return jnp.einsum('jgph,gk->jgpkh', t, eye).reshape(nb, gl * p, gl * h)

    bcat = jnp.concatenate([in_blocks(bbar_re), in_blocks(bbar_im)], axis=-1).astype(BF16)
    ccat = jnp.concatenate([out_blocks(c_re), -out_blocks(c_im)], axis=1).astype(BF16)
    bro = lambda a: jnp.broadcast_to(a.reshape(nb, 1, gl * p), (nb, bsz, gl * p))
    return bcat, ccat, bro(ar), bro(ai), d.astype(F32).reshape(1, g * h)


def _s5(u, bcat, ccat, ar, ai, d, glu_w, glu_b):
    bsz, seq, width = u.shape
    nb, _, ns2 = bcat.shape
    blk = pl.BlockSpec((bsz, S5_T, width), lambda t: (0, t, 0))
    return pl.pallas_call(
        _s5_kernel,
        grid=(seq // S5_T,),
        in_specs=[blk, _const_spec(bcat.shape), _const_spec(ccat.shape), _const_spec(ar.shape),
                  _const_spec(ai.shape), _const_spec((1, width)), _const_spec((width, width)),
                  _const_spec((1, width))],
        out_specs=blk,
        out_shape=jax.ShapeDtypeStruct((bsz, seq, width), BF16),
        scratch_shapes=[pltpu.VMEM((2, ns2 // LANES, bsz * S5_PITCH, LANES), F32),
                        pltpu.VMEM((nb, 2, bsz, ns2 // 2), F32),
                        pltpu.VMEM((bsz * S5_T, width), F32)],
        compiler_params=_params("arbitrary"),
        name="s5_scan",
    )(u, bcat, ccat, ar, ai, d, glu_w.astype(BF16), glu_b.reshape(1, width))


def _mix_kernel(x_ref, ya_ref, yb_ref, yc_ref, wg_ref, wb_ref, wo_ref, g_ref, b_ref, o_ref, *, alpha):
    x = x_ref[...]
    xb = x.astype(BF16)
    d = x.shape[1]
    mix = None
    for k, y_ref in enumerate((ya_ref, yb_ref, yc_ref)):
        gate = jax.nn.sigmoid(_dot(xb, wg_ref[:, k * d:(k + 1) * d]))
        term = gate * _dot(y_ref[...], wb_ref[k])
        mix = term if mix is None else mix + term
    out = _dot(mix.astype(BF16), wo_ref[...])
    o_ref[...] = _layer_norm(alpha * x + out, g_ref[...], b_ref[...])


def _mix(x, ya, yb, yc, w_in, w_branch, w_out, g, b, alpha, layer):
    n, d = x.shape
    width = ya.shape[1]
    row = lambda wd_: pl.BlockSpec((TM, wd_), lambda i: (i, 0))
    assert w_in.shape[2] == 2 * N_BRANCH * d
    return pl.pallas_call(
        functools.partial(_mix_kernel, alpha=alpha),
        grid=(n // TM,),
        in_specs=[row(d), row(width), row(width), row(width),
                  _layer_spec((d, N_BRANCH * d), layer, col_block=1),
                  _layer_spec(w_branch.shape[1:], layer), _layer_spec((d, d), layer),
                  _const_spec((1, d)), _const_spec((1, d))],
        out_specs=row(d),
        out_shape=jax.ShapeDtypeStruct((n, d), F32),
        compiler_params=_params("parallel"),
        name="branch_mix",
    )(x, ya, yb, yc, w_in, w_branch, w_out, g.reshape(1, d), b.reshape(1, d))


def _head_major_qk(w_qk):
    d, cols = w_qk.shape
    heads = cols // (4 * DIFF_QK_DIM)
    w = w_qk.reshape(d, 2, 2, heads, DIFF_QK_DIM)
    return w.transpose(0, 1, 3, 2, 4).reshape(d, cols)


def kernel(x, ffn1_gate, ffn1_up, ffn1_down, ln1_g, ln1_b, w_in, gmlp_ln_g, gmlp_ln_b, gmlp_ws, gmlp_bs, diff_lq1, diff_lk1, diff_lq2, diff_lk2, diff_subln_g, s5_lambda_re, s5_lambda_im, s5_log_step, s5_b_re, s5_b_im, s5_c_re, s5_c_im, s5_d, s5_glu_w, s5_glu_b, w_branch, w_out, ln2_g, ln2_b, ffn2_gate, ffn2_up, ffn2_down, ln3_g, ln3_b):
    bsz, seq, d = x.shape
    depth = w_in.shape[0]
    width = w_branch.shape[2]
    alpha = (2 * depth) ** 0.25
    n = bsz * seq
    xf = x.reshape(n, d)
    cast = lambda a: a.astype(BF16)
    ffn1 = (cast(ffn1_gate), cast(ffn1_up), cast(ffn1_down))
    ffn2 = (cast(ffn2_gate), cast(ffn2_up), cast(ffn2_down))
    w_in_b, w_branch_b, w_out_b = cast(w_in), cast(w_branch), cast(w_out)
    for i in range(depth):
        xf = _ffn(xf, *ffn1, ln1_g[i], ln1_b[i], alpha, i)

        w_qk = _head_major_qk(w_in_b[i, :, 2 * width:4 * width])
        w_vt = w_in_b[i, :, 4 * width:5 * width].T
        bs_full = jnp.broadcast_to(gmlp_bs[i][:, :, None], (GMLP_GROUPS, CHUNK, width // GMLP_GROUPS))
        ya, q, k, vt, zs = _in_proj(xf, w_in_b, w_qk, w_vt, gmlp_ln_g[i], gmlp_ln_b[i], gmlp_ws[i], bs_full,
                                    bsz, seq, width, i)

        lam_init = 0.8 - 0.6 * math.exp(-0.3 * i)
        lam = (jnp.exp(jnp.sum(diff_lq1[i].astype(F32) * diff_lk1[i].astype(F32)))
               - jnp.exp(jnp.sum(diff_lq2[i].astype(F32) * diff_lk2[i].astype(F32))) + lam_init)
        yb = _diff_attention(q, k, vt, lam, diff_subln_g[i], 1.0 - lam_init, bsz, seq)

        s5p = _s5_params(s5_lambda_re[i], s5_lambda_im[i], s5_log_step[i], s5_b_re[i], s5_b_im[i],
                         s5_c_re[i], s5_c_im[i], s5_d[i], bsz)
        yc = _s5(zs.reshape(bsz, seq, width), *s5p, s5_glu_w[i], s5_glu_b[i])

        xf = _mix(xf, ya, yb, yc.reshape(n, width), w_in_b, w_branch_b, w_out_b,
                  ln2_g[i], ln2_b[i], alpha, i)

        xf = _ffn(xf, *ffn2, ln3_g[i], ln3_b[i], alpha, i)
    return xf.reshape(bsz, seq, d)
```

```python
import functools
import math

import jax
import jax.numpy as jnp
from jax import lax
from jax.experimental import pallas as pl
from jax.experimental.pallas import tpu as pltpu

F32 = jnp.float32
BF16 = jnp.bfloat16

CHUNK = 128
GMLP_GROUPS = 4
DIFF_QK_DIM = 64
DIFF_V_DIM = 128
S5_GROUP_DIM = 16
S5_STATE = 64
N_BRANCH = 3
LN_EPS = 1e-5

LANES = 128
SUBLANES = 8
VMEM_LIMIT = 56 * 1024 * 1024

TM = 1024
TF = 256
TQ = 512
S5_T = 128
S5_PITCH = S5_T + SUBLANES


def _layer_norm(y, g, b):
    mu = jnp.mean(y, axis=-1, keepdims=True)
    d = y - mu
    var = jnp.mean(d * d, axis=-1, keepdims=True)
    return d * lax.rsqrt(var + LN_EPS) * g + b


def _dot(a, b):
    return jnp.dot(a, b, preferred_element_type=F32)


def _const_spec(shape):
    zeros = (0,) * len(shape)
    return pl.BlockSpec(shape, lambda *_: zeros, pipeline_mode=pl.Buffered(1))


def _layer_spec(shape, layer, col_block=0):
    idx = (layer,) + (0,) * (len(shape) - 1) + (col_block,)
    return pl.BlockSpec((None,) + tuple(shape), lambda *_: idx, pipeline_mode=pl.Buffered(1))


def _params(*sem):
    return pltpu.CompilerParams(dimension_semantics=sem, vmem_limit_bytes=VMEM_LIMIT)


def _ffn_kernel(x_ref, wg_ref, wu_ref, wd_ref, g_ref, b_ref, o_ref, h_ref, *, alpha):
    x = x_ref[...]
    xb = x.astype(BF16)
    d_ff = wg_ref.shape[1]
    for c in range(d_ff // TF):
        cols = slice(c * TF, (c + 1) * TF)
        gate = _dot(xb, wg_ref[:, cols])
        up = _dot(xb, wu_ref[:, cols])
        h_ref[:, cols] = (gate * jax.nn.sigmoid(gate) * up).astype(BF16)
    out = _dot(h_ref[...], wd_ref[...])
    o_ref[...] = _layer_norm(alpha * x + 0.5 * out, g_ref[...], b_ref[...])


def _ffn(x, wg, wu, wd, g, b, alpha, layer):
    n, d = x.shape
    d_ff = wg.shape[2]
    row = pl.BlockSpec((TM, d), lambda i: (i, 0))
    return pl.pallas_call(
        functools.partial(_ffn_kernel, alpha=alpha),
        grid=(n // TM,),
        in_specs=[row, _layer_spec((d, d_ff), layer), _layer_spec((d, d_ff), layer),
                  _layer_spec((d_ff, d), layer), _const_spec((1, d)), _const_spec((1, d))],
        out_specs=row,
        out_shape=jax.ShapeDtypeStruct((n, d), F32),
        scratch_shapes=[pltpu.VMEM((TM, d_ff), BF16)],
        compiler_params=_params("parallel"),
        name="ffn",
    )(x, wg, wu, wd, g.reshape(1, d), b.reshape(1, d))


def _in_kernel(x_ref, wgm_ref, wqk_ref, wvt_ref, ws5_ref, lng_ref, lnb_ref, ws_ref, bs_ref,
               ya_ref, q_ref, k_ref, vt_ref, zs_ref, *, width, qk_scale):
    xb = x_ref[...].astype(BF16)
    tm = xb.shape[0]

    zg = jax.nn.gelu(_dot(xb, wgm_ref[...]))
    u = zg[:, :width]
    vn = _layer_norm(zg[:, width:], lng_ref[...], lnb_ref[...]).astype(BF16)
    t_idx = lax.broadcasted_iota(jnp.int32, (CHUNK, CHUNK), 0)
    s_idx = lax.broadcasted_iota(jnp.int32, (CHUNK, CHUNK), 1)
    causal = s_idx <= t_idx
    gd = width // GMLP_GROUPS
    for g in range(GMLP_GROUPS):
        ws = jnp.where(causal, ws_ref[g], 0.0).astype(BF16)
        cols = slice(g * gd, (g + 1) * gd)
        for c in range(tm // CHUNK):
            rows = slice(c * CHUNK, (c + 1) * CHUNK)
            s = _dot(ws, vn[rows, cols]) + bs_ref[g]
            ya_ref[rows, cols] = (u[rows, cols] * s).astype(BF16)

    zqk = _dot(xb, wqk_ref[...])
    q_ref[...] = (zqk[:, :width] * qk_scale).astype(BF16)
    k_ref[...] = zqk[:, width:].astype(BF16)
    nt = (((1,), (1,)), ((), ()))
    vt_ref[...] = lax.dot_general(wvt_ref[...], xb, nt, preferred_element_type=F32).astype(BF16)
    zs_ref[...] = _dot(xb, ws5_ref[...])


def _in_proj(x, w_in, wqk, wvt, lng, lnb, ws, bs, bsz, seq, width, layer):
    n, d = x.shape
    lt = seq // TM
    row = lambda wd_: pl.BlockSpec((TM, wd_), lambda i: (i, 0))
    out_bf = jax.ShapeDtypeStruct((n, width), BF16)
    return pl.pallas_call(
        functools.partial(_in_kernel, width=width, qk_scale=math.log2(math.e) * DIFF_QK_DIM ** -0.5),
        grid=(n // TM,),
        in_specs=[row(d), _layer_spec((d, 2 * width), layer), _const_spec(wqk.shape), _const_spec(wvt.shape),
                  _layer_spec((d, width), layer, col_block=5),
                  _const_spec((1, width)), _const_spec((1, width)), _const_spec(ws.shape),
                  _const_spec(bs.shape)],
        out_specs=[row(width), row(width), row(width),
                   pl.BlockSpec((width, TM), lambda i: (i // lt, i % lt)),
                   row(width)],
        out_shape=[out_bf, out_bf, out_bf,
                   jax.ShapeDtypeStruct((bsz * width, seq), BF16),
                   jax.ShapeDtypeStruct((n, width), F32)],
        compiler_params=_params("parallel"),
        name="in_proj_gmlp",
    )(x, w_in, wqk, wvt, w_in, lng.reshape(1, width), lnb.reshape(1, width), ws, bs)


def _attn_kernel(lam_ref, q_ref, k_ref, vt_ref, g_ref, o_ref, acc1_ref, acc2_ref, *, out_scale):
    q = q_ref[...]
    lane = lax.broadcasted_iota(jnp.int32, q.shape, 1)
    zero = jnp.zeros_like(q)
    q1 = jnp.where(lane < DIFF_QK_DIM, q, zero)
    q2 = jnp.where(lane >= DIFF_QK_DIM, q, zero)
    nt = (((1,), (1,)), ((), ()))

    def softmax_step(s, m, l, mask):
        if mask is not None:
            s = jnp.where(mask, s, -jnp.inf)
        m_new = jnp.maximum(m, jnp.max(s, axis=0, keepdims=True))
        p = jnp.exp2(s - m_new)
        corr = jnp.exp2(m - m_new)
        l_new = corr * l + jnp.sum(p, axis=0, keepdims=True)
        return m_new, l_new, corr, p.astype(BF16)

    hq = TQ // 2
    halves = (slice(0, hq), slice(hq, TQ))
    streams = [(h, qm, acc) for h in (0, 1) for qm, acc in ((q1, acc1_ref), (q2, acc2_ref))]

    def scores(tile, j):
        out = []
        for h, qm, _ in streams:
            tk = hq if (j == tile and h == 0) else TQ
            rows = slice(tile * TQ + h * hq, tile * TQ + (h + 1) * hq)
            out.append(lax.dot_general(k_ref[j * TQ:j * TQ + tk, :], qm[rows], nt,
                                       preferred_element_type=F32))
        return out

    def consume(j, s, stats, diagonal):
        new, corr, p = [], [], []
        for n, (h, _, _) in enumerate(streams):
            m, l = stats[n]
            mask = (causal_lo if h == 0 else causal_hi) if diagonal else None
            m, l, c, pn = softmax_step(s[n], m, l, mask)
            new.append((m, l)); corr.append(c); p.append(pn)
        pv = [_dot(vt_ref[:, j * TQ:j * TQ + pn.shape[0]], pn) for pn in p]
        for n, (h, _, acc) in enumerate(streams):
            acc[:, halves[h]] = corr[n] * acc[:, halves[h]] + pv[n]
        return new

    iota = lambda rows, axis: lax.broadcasted_iota(jnp.int32, (rows, hq), axis)
    causal_lo = iota(hq, 0) <= iota(hq, 1)
    causal_hi = iota(TQ, 0) <= iota(TQ, 1) + hq

    def run(diag):
        acc1_ref[...] = jnp.zeros_like(acc1_ref)
        acc2_ref[...] = jnp.zeros_like(acc2_ref)
        stats = [(jnp.full((1, hq), -jnp.inf, F32), jnp.zeros((1, hq), F32))] * len(streams)
        s_next = scores(diag, 0)
        for j in range(diag + 1):
            s_cur = s_next
            if j < diag:
                s_next = scores(diag, j + 1)
            stats = consume(j, s_cur, stats, j == diag)
        for h in (0, 1):
            l1, l2 = stats[2 * h][1], stats[2 * h + 1][1]
            ot = acc1_ref[:, halves[h]] * (1.0 / l1) - acc2_ref[:, halves[h]] * (lam_ref[0] / l2)
            o = ot.T
            o = o * lax.rsqrt(jnp.mean(o * o, axis=-1, keepdims=True) + LN_EPS) * g_ref[...] * out_scale
            o_ref[diag * TQ + h * hq:diag * TQ + (h + 1) * hq, :] = o.astype(BF16)

    for diag in range(k_ref.shape[0] // TQ):
        run(diag)


def _diff_attention(q, k, vt, lam, subln_g, out_scale, bsz, seq):
    n, width = q.shape
    heads = width // DIFF_V_DIM
    qk_spec = pl.BlockSpec((seq, DIFF_V_DIM), lambda b, h: (b, h))
    vt_spec = pl.BlockSpec((DIFF_V_DIM, seq), lambda b, h: (b * heads + h, 0))
    return pl.pallas_call(
        functools.partial(_attn_kernel, out_scale=out_scale),
        grid=(bsz, heads),
        in_specs=[pl.BlockSpec(memory_space=pltpu.SMEM), qk_spec, qk_spec, vt_spec,
                  pl.BlockSpec((1, DIFF_V_DIM), lambda b, h: (0, 0))],
        out_specs=qk_spec,
        out_shape=jax.ShapeDtypeStruct((n, width), BF16),
        scratch_shapes=[pltpu.VMEM((DIFF_V_DIM, TQ), F32), pltpu.VMEM((DIFF_V_DIM, TQ), F32)],
        compiler_params=_params("parallel", "parallel"),
        name="diff_attention",
    )(lam.reshape(1), q, k, vt, subln_g.reshape(1, DIFF_V_DIM))


def _s5_kernel(u_ref, bcat_ref, ccat_ref, ar_ref, ai_ref, d_ref, gw_ref, gb_ref,
               o_ref, h0_ref, h1_ref, st_ref, y_ref):
    @pl.when(pl.program_id(0) == 0)
    def _():
        st_ref[...] = jnp.zeros_like(st_ref)

    bsz, steps, width = u_ref.shape
    u = u_ref[...].reshape(bsz * steps, width)
    ub = u.astype(BF16)
    n_blk = bcat_ref.shape[0]
    slabs = (h0_ref, h1_ref)
    n_slab = h0_ref.shape[0]
    half = n_slab // 2
    pitch = h0_ref.shape[1] // bsz

    def slab_rows(buf, k, b):
        return slabs[buf].at[k, b * pitch:b * pitch + steps, :]

    def project_in(j):
        bu = _dot(ub[:, j * LANES:(j + 1) * LANES], bcat_ref[j])
        for k in range(n_slab):
            for b in range(bsz):
                slab_rows(j % 2, k, b)[...] = bu[b * steps:(b + 1) * steps, k * LANES:(k + 1) * LANES]

    project_in(0)
    for j in range(n_blk):
        if j + 1 < n_blk:
            project_in(j + 1)
        buf = j % 2
        ar = [ar_ref[j, :, k * LANES:(k + 1) * LANES] for k in range(half)]
        ai = [ai_ref[j, :, k * LANES:(k + 1) * LANES] for k in range(half)]
        hr = [st_ref[j, 0, :, k * LANES:(k + 1) * LANES] for k in range(half)]
        hi = [st_ref[j, 1, :, k * LANES:(k + 1) * LANES] for k in range(half)]
        for t in range(steps):
            at_t = pl.ds(t, bsz, stride=pitch)
            for k in range(half):
                nhr = ar[k] * hr[k] - ai[k] * hi[k] + slabs[buf][k, at_t, :]
                nhi = ar[k] * hi[k] + ai[k] * hr[k] + slabs[buf][half + k, at_t, :]
                slabs[buf][k, at_t, :] = nhr
                slabs[buf][half + k, at_t, :] = nhi
                hr[k], hi[k] = nhr, nhi
        for k in range(half):
            st_ref[j, 0, :, k * LANES:(k + 1) * LANES] = hr[k]
            st_ref[j, 1, :, k * LANES:(k + 1) * LANES] = hi[k]
        h = jnp.concatenate(
            [jnp.concatenate([slab_rows(buf, k, b)[...] for k in range(n_slab)], axis=1) for b in range(bsz)],
            axis=0)
        y_ref[:, j * LANES:(j + 1) * LANES] = _dot(h.astype(BF16), ccat_ref[j])

    y = jax.nn.gelu(y_ref[...] + d_ref[...] * u)
    gate = jax.nn.sigmoid(_dot(y.astype(BF16), gw_ref[...]) + gb_ref[...])
    o_ref[...] = (y * gate).astype(BF16).reshape(bsz, steps, width)


def _s5_params(lam_re, lam_im, log_step, b_re, b_im, c_re, c_im, d, bsz):
    g, p = lam_re.shape
    h = b_re.shape[-1]
    gl = LANES // h
    nb = g // gl
    lr = lam_re.astype(F32)
    li = lam_im.astype(F32)
    step = jnp.exp(log_step.astype(F32))[:, None]
    mag = jnp.exp(lr * step)
    ar = mag * jnp.cos(li * step)
    ai = mag * jnp.sin(li * step)
    den = lr * lr + li * li
    cr = ((ar - 1.0) * lr + ai * li) / den
    ci = (ai * lr - (ar - 1.0) * li) / den
    br = b_re.astype(F32)
    bi = b_im.astype(F32)
    bbar_re = cr[..., None] * br - ci[..., None] * bi
    bbar_im = cr[..., None] * bi + ci[..., None] * br
    eye = jnp.eye(gl, dtype=F32)

    def in_blocks(bb):
        t = bb.transpose(0, 2, 1).reshape(nb, gl, h, p)
        return jnp.einsum('jghp,gk->jghkp', t, eye).reshape(nb, gl * h, gl * p)

    def out_blocks(cc):
        t = cc.astype(F32).transpose(0, 2, 1).reshape(nb, gl, p, h)
        return jnp.einsum('jgph,gk->jgpkh', t, eye).reshape(nb, gl * p, gl * h)

    bcat = jnp.concatenate([in_blocks(bbar_re), in_blocks(bbar_im)], axis=-1).astype(BF16)
    ccat = jnp.concatenate([out_blocks(c_re), -out_blocks(c_im)], axis=1).astype(BF16)
    bro = lambda a: jnp.broadcast_to(a.reshape(nb, 1, gl * p), (nb, bsz, gl * p))
    return bcat, ccat, bro(ar), bro(ai), d.astype(F32).reshape(1, g * h)


def _s5(u, bcat, ccat, ar, ai, d, glu_w, glu_b):
    bsz, seq, width = u.shape
    nb, _, ns2 = bcat.shape
    blk = pl.BlockSpec((bsz, S5_T, width), lambda t: (0, t, 0))
    return pl.pallas_call(
        _s5_kernel,
        grid=(seq // S5_T,),
        in_specs=[blk, _const_spec(bcat.shape), _const_spec(ccat.shape), _const_spec(ar.shape),
                  _const_spec(ai.shape), _const_spec((1, width)), _const_spec((width, width)),
                  _const_spec((1, width))],
        out_specs=blk,
        out_shape=jax.ShapeDtypeStruct((bsz, seq, width), BF16),
        scratch_shapes=[pltpu.VMEM((ns2 // LANES, bsz * S5_PITCH, LANES), F32),
                        pltpu.VMEM((ns2 // LANES, bsz * S5_PITCH, LANES), F32),
                        pltpu.VMEM((nb, 2, bsz, ns2 // 2), F32),
                        pltpu.VMEM((bsz * S5_T, width), F32)],
        compiler_params=_params("arbitrary"),
        name="s5_scan",
    )(u, bcat, ccat, ar, ai, d, glu_w.astype(BF16), glu_b.reshape(1, width))


def _mix_kernel(x_ref, ya_ref, yb_ref, yc_ref, wg_ref, wb_ref, wo_ref, g_ref, b_ref, o_ref, m_ref, *, alpha):
    x = x_ref[...]
    xb = x.astype(BF16)
    d = x.shape[1]
    ys = [y_ref[...] for y_ref in (ya_ref, yb_ref, yc_ref)]
    for c in range(d // TF):
        cols = slice(c * TF, (c + 1) * TF)
        mix = None
        for k, y in enumerate(ys):
            gate = jax.nn.sigmoid(_dot(xb, wg_ref[:, k * d + c * TF:k * d + (c + 1) * TF]))
            term = gate * _dot(y, wb_ref[k, :, cols])
            mix = term if mix is None else mix + term
        m_ref[:, cols] = mix.astype(BF16)
    out = _dot(m_ref[...], wo_ref[...])
    o_ref[...] = _layer_norm(alpha * x + out, g_ref[...], b_ref[...])


def _mix(x, ya, yb, yc, w_in, w_branch, w_out, g, b, alpha, layer):
    n, d = x.shape
    width = ya.shape[1]
    row = lambda wd_: pl.BlockSpec((TM, wd_), lambda i: (i, 0))
    assert w_in.shape[2] == 2 * N_BRANCH * d
    return pl.pallas_call(
        functools.partial(_mix_kernel, alpha=alpha),
        grid=(n // TM,),
        in_specs=[row(d), row(width), row(width), row(width),
                  _layer_spec((d, N_BRANCH * d), layer, col_block=1),
                  _layer_spec(w_branch.shape[1:], layer), _layer_spec((d, d), layer),
                  _const_spec((1, d)), _const_spec((1, d))],
        out_specs=row(d),
        out_shape=jax.ShapeDtypeStruct((n, d), F32),
        scratch_shapes=[pltpu.VMEM((TM, d), BF16)],
        compiler_params=_params("parallel"),
        name="branch_mix",
    )(x, ya, yb, yc, w_in, w_branch, w_out, g.reshape(1, d), b.reshape(1, d))


def _head_major_qk(w_qk):
    d, cols = w_qk.shape
    heads = cols // (4 * DIFF_QK_DIM)
    w = w_qk.reshape(d, 2, 2, heads, DIFF_QK_DIM)
    return w.transpose(0, 1, 3, 2, 4).reshape(d, cols)


def kernel(x, ffn1_gate, ffn1_up, ffn1_down, ln1_g, ln1_b, w_in, gmlp_ln_g, gmlp_ln_b, gmlp_ws, gmlp_bs, diff_lq1, diff_lk1, diff_lq2, diff_lk2, diff_subln_g, s5_lambda_re, s5_lambda_im, s5_log_step, s5_b_re, s5_b_im, s5_c_re, s5_c_im, s5_d, s5_glu_w, s5_glu_b, w_branch, w_out, ln2_g, ln2_b, ffn2_gate, ffn2_up, ffn2_down, ln3_g, ln3_b):
    bsz, seq, d = x.shape
    depth = w_in.shape[0]
    width = w_branch.shape[2]
    alpha = (2 * depth) ** 0.25
    n = bsz * seq
    xf = x.reshape(n, d)
    cast = lambda a: a.astype(BF16)
    ffn1 = (cast(ffn1_gate), cast(ffn1_up), cast(ffn1_down))
    ffn2 = (cast(ffn2_gate), cast(ffn2_up), cast(ffn2_down))
    w_in_b, w_branch_b, w_out_b = cast(w_in), cast(w_branch), cast(w_out)
    for i in range(depth):
        xf = _ffn(xf, *ffn1, ln1_g[i], ln1_b[i], alpha, i)

        w_qk = _head_major_qk(w_in_b[i, :, 2 * width:4 * width])
        w_vt = w_in_b[i, :, 4 * width:5 * width].T
        bs_full = jnp.broadcast_to(gmlp_bs[i][:, :, None], (GMLP_GROUPS, CHUNK, width // GMLP_GROUPS))
        ya, q, k, vt, zs = _in_proj(xf, w_in_b, w_qk, w_vt, gmlp_ln_g[i], gmlp_ln_b[i], gmlp_ws[i], bs_full,
                                    bsz, seq, width, i)

        lam_init = 0.8 - 0.6 * math.exp(-0.3 * i)
        lam = (jnp.exp(jnp.sum(diff_lq1[i].astype(F32) * diff_lk1[i].astype(F32)))
               - jnp.exp(jnp.sum(diff_lq2[i].astype(F32) * diff_lk2[i].astype(F32))) + lam_init)
        yb = _diff_attention(q, k, vt, lam, diff_subln_g[i], 1.0 - lam_init, bsz, seq)

        s5p = _s5_params(s5_lambda_re[i], s5_lambda_im[i], s5_log_step[i], s5_b_re[i], s5_b_im[i],
                         s5_c_re[i], s5_c_im[i], s5_d[i], bsz)
        yc = _s5(zs.reshape(bsz, seq, width), *s5p, s5_glu_w[i], s5_glu_b[i])

        xf = _mix(xf, ya, yb, yc.reshape(n, width), w_in_b, w_branch_b, w_out_b,
                  ln2_g[i], ln2_b[i], alpha, i)

        xf = _ffn(xf, *ffn2, ln3_g[i], ln3_b[i], alpha, i)
    return xf.reshape(bsz, seq, d)
```

```python
import functools
import math

import jax
import jax.numpy as jnp
from jax import lax
from jax.experimental import pallas as pl
from jax.experimental.pallas import tpu as pltpu

F32 = jnp.float32
BF16 = jnp.bfloat16

CHUNK = 128
GMLP_GROUPS = 4
DIFF_QK_DIM = 64
DIFF_V_DIM = 128
S5_GROUP_DIM = 16
S5_STATE = 64
N_BRANCH = 3
LN_EPS = 1e-5

LANES = 128
SUBLANES = 8
VMEM_LIMIT = 56 * 1024 * 1024

TM = 1024
TF = 256
ROW_GROUPS = 4
TQ = 512
S5_T = 128
S5_PITCH = S5_T + SUBLANES


def _layer_norm(y, g, b):
    mu = jnp.mean(y, axis=-1, keepdims=True)
    d = y - mu
    var = jnp.mean(d * d, axis=-1, keepdims=True)
    return d * lax.rsqrt(var + LN_EPS) * g + b


def _dot(a, b):
    return jnp.dot(a, b, preferred_element_type=F32)


def _const_spec(shape):
    zeros = (0,) * len(shape)
    return pl.BlockSpec(shape, lambda *_: zeros, pipeline_mode=pl.Buffered(1))


def _layer_spec(shape, layer, col_block=0):
    idx = (layer,) + (0,) * (len(shape) - 1) + (col_block,)
    return pl.BlockSpec((None,) + tuple(shape), lambda *_: idx, pipeline_mode=pl.Buffered(1))


def _params(*sem):
    return pltpu.CompilerParams(dimension_semantics=sem, vmem_limit_bytes=VMEM_LIMIT)


def _ffn_kernel(x_ref, wg_ref, wu_ref, wd_ref, g_ref, b_ref, o_ref, h_ref, *, alpha):
    x = x_ref[...]
    xb = x.astype(BF16)
    d_ff = wg_ref.shape[1]
    for c in range(d_ff // TF):
        cols = slice(c * TF, (c + 1) * TF)
        gate = _dot(xb, wg_ref[:, cols])
        up = _dot(xb, wu_ref[:, cols])
        h_ref[:, cols] = (gate * jax.nn.sigmoid(gate) * up).astype(BF16)
    rg = x.shape[0] // ROW_GROUPS
    for r in range(ROW_GROUPS):
        rows = slice(r * rg, (r + 1) * rg)
        out = _dot(h_ref[rows, :], wd_ref[...])
        o_ref[rows, :] = _layer_norm(alpha * x[rows] + 0.5 * out, g_ref[...], b_ref[...])


def _ffn(x, wg, wu, wd, g, b, alpha, layer):
    n, d = x.shape
    d_ff = wg.shape[2]
    row = pl.BlockSpec((TM, d), lambda i: (i, 0))
    return pl.pallas_call(
        functools.partial(_ffn_kernel, alpha=alpha),
        grid=(n // TM,),
        in_specs=[row, _layer_spec((d, d_ff), layer), _layer_spec((d, d_ff), layer),
                  _layer_spec((d_ff, d), layer), _const_spec((1, d)), _const_spec((1, d))],
        out_specs=row,
        out_shape=jax.ShapeDtypeStruct((n, d), F32),
        scratch_shapes=[pltpu.VMEM((TM, d_ff), BF16)],
        compiler_params=_params("parallel"),
        name="ffn",
    )(x, wg, wu, wd, g.reshape(1, d), b.reshape(1, d))


def _in_kernel(x_ref, wgm_ref, wqk_ref, wvt_ref, ws5_ref, lng_ref, lnb_ref, ws_ref, bs_ref,
               ya_ref, q_ref, k_ref, vt_ref, zs_ref, *, width, qk_scale):
    xb = x_ref[...].astype(BF16)
    tm = xb.shape[0]

    z_gmlp = _dot(xb, wgm_ref[...])
    zqk = _dot(xb, wqk_ref[...])
    q_ref[...] = (zqk[:, :width] * qk_scale).astype(BF16)
    k_ref[...] = zqk[:, width:].astype(BF16)
    nt = (((1,), (1,)), ((), ()))
    vt_ref[...] = lax.dot_general(wvt_ref[...], xb, nt, preferred_element_type=F32).astype(BF16)
    zs_ref[...] = _dot(xb, ws5_ref[...])

    zg = jax.nn.gelu(z_gmlp)
    u = zg[:, :width]
    vn = _layer_norm(zg[:, width:], lng_ref[...], lnb_ref[...]).astype(BF16)
    t_idx = lax.broadcasted_iota(jnp.int32, (CHUNK, CHUNK), 0)
    s_idx = lax.broadcasted_iota(jnp.int32, (CHUNK, CHUNK), 1)
    causal = s_idx <= t_idx
    gd = width // GMLP_GROUPS
    for g in range(GMLP_GROUPS):
        ws = jnp.where(causal, ws_ref[g], 0.0).astype(BF16)
        cols = slice(g * gd, (g + 1) * gd)
        for c in range(tm // CHUNK):
            rows = slice(c * CHUNK, (c + 1) * CHUNK)
            s = _dot(ws, vn[rows, cols]) + bs_ref[g]
            ya_ref[rows, cols] = (u[rows, cols] * s).astype(BF16)


def _in_proj(x, w_in, wqk, wvt, lng, lnb, ws, bs, bsz, seq, width, layer):
    n, d = x.shape
    lt = seq // TM
    row = lambda wd_: pl.BlockSpec((TM, wd_), lambda i: (i, 0))
    out_bf = jax.ShapeDtypeStruct((n, width), BF16)
    return pl.pallas_call(
        functools.partial(_in_kernel, width=width, qk_scale=math.log2(math.e) * DIFF_QK_DIM ** -0.5),
        grid=(n // TM,),
        in_specs=[row(d), _layer_spec((d, 2 * width), layer), _const_spec(wqk.shape), _const_spec(wvt.shape),
                  _layer_spec((d, width), layer, col_block=5),
                  _const_spec((1, width)), _const_spec((1, width)), _const_spec(ws.shape),
                  _const_spec(bs.shape)],
        out_specs=[row(width), row(width), row(width),
                   pl.BlockSpec((width, TM), lambda i: (i // lt, i % lt)),
                   row(width)],
        out_shape=[out_bf, out_bf, out_bf,
                   jax.ShapeDtypeStruct((bsz * width, seq), BF16),
                   jax.ShapeDtypeStruct((n, width), F32)],
        compiler_params=_params("parallel"),
        name="in_proj_gmlp",
    )(x, w_in, wqk, wvt, w_in, lng.reshape(1, width), lnb.reshape(1, width), ws, bs)


def _attn_kernel(lam_ref, q_ref, k_ref, vt_ref, g_ref, o_ref, acc1_ref, acc2_ref, *, out_scale):
    q = q_ref[...]
    lane = lax.broadcasted_iota(jnp.int32, q.shape, 1)
    zero = jnp.zeros_like(q)
    q1 = jnp.where(lane < DIFF_QK_DIM, q, zero)
    q2 = jnp.where(lane >= DIFF_QK_DIM, q, zero)
    nt = (((1,), (1,)), ((), ()))

    def softmax_step(s, m, l, mask):
        if mask is not None:
            s = jnp.where(mask, s, -jnp.inf)
        m_new = jnp.maximum(m, jnp.max(s, axis=0, keepdims=True))
        p = jnp.exp2(s - m_new)
        corr = jnp.exp2(m - m_new)
        l_new = corr * l + jnp.sum(p, axis=0, keepdims=True)
        return m_new, l_new, corr, p.astype(BF16)

    hq = TQ // 2
    halves = (slice(0, hq), slice(hq, TQ))
    streams = [(h, qm, acc) for h in (0, 1) for qm, acc in ((q1, acc1_ref), (q2, acc2_ref))]

    def scores(tile, j):
        out = []
        for h, qm, _ in streams:
            tk = hq if (j == tile and h == 0) else TQ
            rows = slice(tile * TQ + h * hq, tile * TQ + (h + 1) * hq)
            out.append(lax.dot_general(k_ref[j * TQ:j * TQ + tk, :], qm[rows], nt,
                                       preferred_element_type=F32))
        return out

    def consume(j, s, stats, diagonal):
        new, corr, p = [], [], []
        for n, (h, _, _) in enumerate(streams):
            m, l = stats[n]
            mask = (causal_lo if h == 0 else causal_hi) if diagonal else None
            m, l, c, pn = softmax_step(s[n], m, l, mask)
            new.append((m, l)); corr.append(c); p.append(pn)
        pv = [_dot(vt_ref[:, j * TQ:j * TQ + pn.shape[0]], pn) for pn in p]
        for n, (h, _, acc) in enumerate(streams):
            acc[:, halves[h]] = corr[n] * acc[:, halves[h]] + pv[n]
        return new

    iota = lambda rows, axis: lax.broadcasted_iota(jnp.int32, (rows, hq), axis)
    causal_lo = iota(hq, 0) <= iota(hq, 1)
    causal_hi = iota(TQ, 0) <= iota(TQ, 1) + hq

    def run(diag):
        acc1_ref[...] = jnp.zeros_like(acc1_ref)
        acc2_ref[...] = jnp.zeros_like(acc2_ref)
        stats = [(jnp.full((1, hq), -jnp.inf, F32), jnp.zeros((1, hq), F32))] * len(streams)
        s_next = scores(diag, 0)
        for j in range(diag + 1):
            s_cur = s_next
            if j < diag:
                s_next = scores(diag, j + 1)
            stats = consume(j, s_cur, stats, j == diag)
        for h in (0, 1):
            l1, l2 = stats[2 * h][1], stats[2 * h + 1][1]
            ot = acc1_ref[:, halves[h]] * (1.0 / l1) - acc2_ref[:, halves[h]] * (lam_ref[0] / l2)
            o = ot.T
            o = o * lax.rsqrt(jnp.mean(o * o, axis=-1, keepdims=True) + LN_EPS) * g_ref[...] * out_scale
            o_ref[diag * TQ + h * hq:diag * TQ + (h + 1) * hq, :] = o.astype(BF16)

    for diag in range(k_ref.shape[0] // TQ):
        run(diag)


def _diff_attention(q, k, vt, lam, subln_g, out_scale, bsz, seq):
    n, width = q.shape
    heads = width // DIFF_V_DIM
    qk_spec = pl.BlockSpec((seq, DIFF_V_DIM), lambda b, h: (b, h))
    vt_spec = pl.BlockSpec((DIFF_V_DIM, seq), lambda b, h: (b * heads + h, 0))
    return pl.pallas_call(
        functools.partial(_attn_kernel, out_scale=out_scale),
        grid=(bsz, heads),
        in_specs=[pl.BlockSpec(memory_space=pltpu.SMEM), qk_spec, qk_spec, vt_spec,
                  pl.BlockSpec((1, DIFF_V_DIM), lambda b, h: (0, 0))],
        out_specs=qk_spec,
        out_shape=jax.ShapeDtypeStruct((n, width), BF16),
        scratch_shapes=[pltpu.VMEM((DIFF_V_DIM, TQ), F32), pltpu.VMEM((DIFF_V_DIM, TQ), F32)],
        compiler_params=_params("parallel", "parallel"),
        name="diff_attention",
    )(lam.reshape(1), q, k, vt, subln_g.reshape(1, DIFF_V_DIM))


def _s5_kernel(u_ref, bcat_ref, ccat_ref, ar_ref, ai_ref, d_ref, gw_ref, gb_ref,
               o_ref, h0_ref, h1_ref, st_ref, y_ref):
    @pl.when(pl.program_id(0) == 0)
    def _():
        st_ref[...] = jnp.zeros_like(st_ref)

    bsz, steps, width = u_ref.shape
    u = u_ref[...].reshape(bsz * steps, width)
    ub = u.astype(BF16)
    n_blk = bcat_ref.shape[0]
    slabs = (h0_ref, h1_ref)
    n_slab = h0_ref.shape[0]
    half = n_slab // 2
    pitch = h0_ref.shape[1] // bsz

    def slab_rows(buf, k, b):
        return slabs[buf].at[k, b * pitch:b * pitch + steps, :]

    def project_in(j):
        bu = _dot(ub[:, j * LANES:(j + 1) * LANES], bcat_ref[j])
        for k in range(n_slab):
            for b in range(bsz):
                slab_rows(j % 2, k, b)[...] = bu[b * steps:(b + 1) * steps, k * LANES:(k + 1) * LANES]

    project_in(0)
    for j in range(n_blk):
        if j + 1 < n_blk:
            project_in(j + 1)
        buf = j % 2
        ar = [ar_ref[j, :, k * LANES:(k + 1) * LANES] for k in range(half)]
        ai = [ai_ref[j, :, k * LANES:(k + 1) * LANES] for k in range(half)]
        hr = [st_ref[j, 0, :, k * LANES:(k + 1) * LANES] for k in range(half)]
        hi = [st_ref[j, 1, :, k * LANES:(k + 1) * LANES] for k in range(half)]
        for t in range(steps):
            at_t = pl.ds(t, bsz, stride=pitch)
            for k in range(half):
                nhr = ar[k] * hr[k] - ai[k] * hi[k] + slabs[buf][k, at_t, :]
                nhi = ar[k] * hi[k] + ai[k] * hr[k] + slabs[buf][half + k, at_t, :]
                slabs[buf][k, at_t, :] = nhr
                slabs[buf][half + k, at_t, :] = nhi
                hr[k], hi[k] = nhr, nhi
        for k in range(half):
            st_ref[j, 0, :, k * LANES:(k + 1) * LANES] = hr[k]
            st_ref[j, 1, :, k * LANES:(k + 1) * LANES] = hi[k]
        h = jnp.concatenate(
            [jnp.concatenate([slab_rows(buf, k, b)[...] for k in range(n_slab)], axis=1) for b in range(bsz)],
            axis=0)
        y_ref[:, j * LANES:(j + 1) * LANES] = _dot(h.astype(BF16), ccat_ref[j])

    y = jax.nn.gelu(y_ref[...] + d_ref[...] * u)
    gate = jax.nn.sigmoid(_dot(y.astype(BF16), gw_ref[...]) + gb_ref[...])
    o_ref[...] = (y * gate).astype(BF16).reshape(bsz, steps, width)


def _s5_params(lam_re, lam_im, log_step, b_re, b_im, c_re, c_im, d, bsz):
    g, p = lam_re.shape
    h = b_re.shape[-1]
    gl = LANES // h
    nb = g // gl
    lr = lam_re.astype(F32)
    li = lam_im.astype(F32)
    step = jnp.exp(log_step.astype(F32))[:, None]
    mag = jnp.exp(lr * step)
    ar = mag * jnp.cos(li * step)
    ai = mag * jnp.sin(li * step)
    den = lr * lr + li * li
    cr = ((ar - 1.0) * lr + ai * li) / den
    ci = (ai * lr - (ar - 1.0) * li) / den
    br = b_re.astype(F32)
    bi = b_im.astype(F32)
    bbar_re = cr[..., None] * br - ci[..., None] * bi
    bbar_im = cr[..., None] * bi + ci[..., None] * br
    eye = jnp.eye(gl, dtype=F32)

    def in_blocks(bb):
        t = bb.transpose(0, 2, 1).reshape(nb, gl, h, p)
        return jnp.einsum('jghp,gk->jghkp', t, eye).reshape(nb, gl * h, gl * p)

    def out_blocks(cc):
        t = cc.astype(F32).transpose(0, 2, 1).reshape(nb, gl, p, h)
        return jnp.einsum('jgph,gk->jgpkh', t, eye).reshape(nb, gl * p, gl * h)

    bcat = jnp.concatenate([in_blocks(bbar_re), in_blocks(bbar_im)], axis=-1).astype(BF16)
    ccat = jnp.concatenate([out_blocks(c_re), -out_blocks(c_im)], axis=1).astype(BF16)
    bro = lambda a: jnp.broadcast_to(a.reshape(nb, 1, gl * p), (nb, bsz, gl * p))
    return bcat, ccat, bro(ar), bro(ai), d.astype(F32).reshape(1, g * h)


def _s5(u, bcat, ccat, ar, ai, d, glu_w, glu_b):
    bsz, seq, width = u.shape
    nb, _, ns2 = bcat.shape
    blk = pl.BlockSpec((bsz, S5_T, width), lambda t: (0, t, 0))
    return pl.pallas_call(
        _s5_kernel,
        grid=(seq // S5_T,),
        in_specs=[blk, _const_spec(bcat.shape), _const_spec(ccat.shape), _const_spec(ar.shape),
                  _const_spec(ai.shape), _const_spec((1, width)), _const_spec((width, width)),
                  _const_spec((1, width))],
        out_specs=blk,
        out_shape=jax.ShapeDtypeStruct((bsz, seq, width), BF16),
        scratch_shapes=[pltpu.VMEM((ns2 // LANES, bsz * S5_PITCH, LANES), F32),
                        pltpu.VMEM((ns2 // LANES, bsz * S5_PITCH, LANES), F32),
                        pltpu.VMEM((nb, 2, bsz, ns2 // 2), F32),
                        pltpu.VMEM((bsz * S5_T, width), F32)],
        compiler_params=_params("arbitrary"),
        name="s5_scan",
    )(u, bcat, ccat, ar, ai, d, glu_w.astype(BF16), glu_b.reshape(1, width))


def _mix_kernel(x_ref, ya_ref, yb_ref, yc_ref, wg_ref, wb_ref, wo_ref, g_ref, b_ref, o_ref, m_ref, *, alpha):
    x = x_ref[...]
    xb = x.astype(BF16)
    d = x.shape[1]
    ys = [y_ref[...] for y_ref in (ya_ref, yb_ref, yc_ref)]
    for c in range(d // TF):
        cols = slice(c * TF, (c + 1) * TF)
        mix = None
        for k, y in enumerate(ys):
            gate = jax.nn.sigmoid(_dot(xb, wg_ref[:, k * d + c * TF:k * d + (c + 1) * TF]))
            term = gate * _dot(y, wb_ref[k, :, cols])
            mix = term if mix is None else mix + term
        m_ref[:, cols] = mix.astype(BF16)
    rg = x.shape[0] // ROW_GROUPS
    for r in range(ROW_GROUPS):
        rows = slice(r * rg, (r + 1) * rg)
        out = _dot(m_ref[rows, :], wo_ref[...])
        o_ref[rows, :] = _layer_norm(alpha * x[rows] + out, g_ref[...], b_ref[...])


def _mix(x, ya, yb, yc, w_in, w_branch, w_out, g, b, alpha, layer):
    n, d = x.shape
    width = ya.shape[1]
    row = lambda wd_: pl.BlockSpec((TM, wd_), lambda i: (i, 0))
    assert w_in.shape[2] == 2 * N_BRANCH * d
    return pl.pallas_call(
        functools.partial(_mix_kernel, alpha=alpha),
        grid=(n // TM,),
        in_specs=[row(d), row(width), row(width), row(width),
                  _layer_spec((d, N_BRANCH * d), layer, col_block=1),
                  _layer_spec(w_branch.shape[1:], layer), _layer_spec((d, d), layer),
                  _const_spec((1, d)), _const_spec((1, d))],
        out_specs=row(d),
        out_shape=jax.ShapeDtypeStruct((n, d), F32),
        scratch_shapes=[pltpu.VMEM((TM, d), BF16)],
        compiler_params=_params("parallel"),
        name="branch_mix",
    )(x, ya, yb, yc, w_in, w_branch, w_out, g.reshape(1, d), b.reshape(1, d))


def _head_major_qk(w_qk):
    d, cols = w_qk.shape
    heads = cols // (4 * DIFF_QK_DIM)
    w = w_qk.reshape(d, 2, 2, heads, DIFF_QK_DIM)
    return w.transpose(0, 1, 3, 2, 4).reshape(d, cols)


def kernel(x, ffn1_gate, ffn1_up, ffn1_down, ln1_g, ln1_b, w_in, gmlp_ln_g, gmlp_ln_b, gmlp_ws, gmlp_bs, diff_lq1, diff_lk1, diff_lq2, diff_lk2, diff_subln_g, s5_lambda_re, s5_lambda_im, s5_log_step, s5_b_re, s5_b_im, s5_c_re, s5_c_im, s5_d, s5_glu_w, s5_glu_b, w_branch, w_out, ln2_g, ln2_b, ffn2_gate, ffn2_up, ffn2_down, ln3_g, ln3_b):
    bsz, seq, d = x.shape
    depth = w_in.shape[0]
    width = w_branch.shape[2]
    alpha = (2 * depth) ** 0.25
    n = bsz * seq
    xf = x.reshape(n, d)
    cast = lambda a: a.astype(BF16)
    ffn1 = (cast(ffn1_gate), cast(ffn1_up), cast(ffn1_down))
    ffn2 = (cast(ffn2_gate), cast(ffn2_up), cast(ffn2_down))
    w_in_b, w_branch_b, w_out_b = cast(w_in), cast(w_branch), cast(w_out)
    for i in range(depth):
        xf = _ffn(xf, *ffn1, ln1_g[i], ln1_b[i], alpha, i)

        w_qk = _head_major_qk(w_in_b[i, :, 2 * width:4 * width])
        w_vt = w_in_b[i, :, 4 * width:5 * width].T
        bs_full = jnp.broadcast_to(gmlp_bs[i][:, :, None], (GMLP_GROUPS, CHUNK, width // GMLP_GROUPS))
        ya, q, k, vt, zs = _in_proj(xf, w_in_b, w_qk, w_vt, gmlp_ln_g[i], gmlp_ln_b[i], gmlp_ws[i], bs_full,
                                    bsz, seq, width, i)

        lam_init = 0.8 - 0.6 * math.exp(-0.3 * i)
        lam = (jnp.exp(jnp.sum(diff_lq1[i].astype(F32) * diff_lk1[i].astype(F32)))
               - jnp.exp(jnp.sum(diff_lq2[i].astype(F32) * diff_lk2[i].astype(F32))) + lam_init)
        yb = _diff_attention(q, k, vt, lam, diff_subln_g[i], 1.0 - lam_init, bsz, seq)

        s5p = _s5_params(s5_lambda_re[i], s5_lambda_im[i], s5_log_step[i], s5_b_re[i], s5_b_im[i],
                         s5_c_re[i], s5_c_im[i], s5_d[i], bsz)
        yc = _s5(zs.reshape(bsz, seq, width), *s5p, s5_glu_w[i], s5_glu_b[i])

        xf = _mix(xf, ya, yb, yc.reshape(n, width), w_in_b, w_branch_b, w_out_b,
                  ln2_g[i], ln2_b[i], alpha, i)

        xf = _ffn(xf, *ffn2, ln3_g[i], ln3_b[i], alpha, i)
    return xf.reshape(bsz, seq, d)
```

```python
import functools
import math

import jax
import jax.numpy as jnp
from jax import lax
from jax.experimental import pallas as pl
from jax.experimental.pallas import tpu as pltpu

F32 = jnp.float32
BF16 = jnp.bfloat16

CHUNK = 128
GMLP_GROUPS = 4
DIFF_QK_DIM = 64
DIFF_V_DIM = 128
S5_GROUP_DIM = 16
S5_STATE = 64
N_BRANCH = 3
LN_EPS = 1e-5

LANES = 128
SUBLANES = 8
VMEM_LIMIT = 56 * 1024 * 1024

TM = 1024
TF = 256
ROW_GROUPS = 4
TQ = 512
S5_T = 128
S5_PITCH = S5_T + SUBLANES


def _layer_norm(y, g, b):
    mu = jnp.mean(y, axis=-1, keepdims=True)
    d = y - mu
    var = jnp.mean(d * d, axis=-1, keepdims=True)
    return d * lax.rsqrt(var + LN_EPS) * g + b


def _dot(a, b):
    return jnp.dot(a, b, preferred_element_type=F32)


def _const_spec(shape):
    zeros = (0,) * len(shape)
    return pl.BlockSpec(shape, lambda *_: zeros, pipeline_mode=pl.Buffered(1))


def _layer_spec(shape, layer, col_block=0):
    idx = (layer,) + (0,) * (len(shape) - 1) + (col_block,)
    return pl.BlockSpec((None,) + tuple(shape), lambda *_: idx, pipeline_mode=pl.Buffered(1))


def _params(*sem):
    return pltpu.CompilerParams(dimension_semantics=sem, vmem_limit_bytes=VMEM_LIMIT)


def _ffn_kernel(x_ref, wg_ref, wu_ref, wd_ref, g_ref, b_ref, o_ref, h_ref, *, alpha):
    x = x_ref[...]
    xb = x.astype(BF16)
    d_ff = wg_ref.shape[1]
    for c in range(d_ff // TF):
        cols = slice(c * TF, (c + 1) * TF)
        gate = _dot(xb, wg_ref[:, cols])
        up = _dot(xb, wu_ref[:, cols])
        h_ref[:, cols] = (gate * jax.nn.sigmoid(gate) * up).astype(BF16)
    rg = x.shape[0] // ROW_GROUPS
    for r in range(ROW_GROUPS):
        rows = slice(r * rg, (r + 1) * rg)
        out = _dot(h_ref[rows, :], wd_ref[...])
        o_ref[rows, :] = _layer_norm(alpha * x[rows] + 0.5 * out, g_ref[...], b_ref[...])


def _ffn(x, wg, wu, wd, g, b, alpha, layer):
    n, d = x.shape
    d_ff = wg.shape[2]
    row = pl.BlockSpec((TM, d), lambda i: (i, 0))
    return pl.pallas_call(
        functools.partial(_ffn_kernel, alpha=alpha),
        grid=(n // TM,),
        in_specs=[row, _layer_spec((d, d_ff), layer), _layer_spec((d, d_ff), layer),
                  _layer_spec((d_ff, d), layer), _const_spec((1, d)), _const_spec((1, d))],
        out_specs=row,
        out_shape=jax.ShapeDtypeStruct((n, d), F32),
        scratch_shapes=[pltpu.VMEM((TM, d_ff), BF16)],
        compiler_params=_params("parallel"),
        name="ffn",
    )(x, wg, wu, wd, g.reshape(1, d), b.reshape(1, d))


def _in_kernel(x_ref, wgm_ref, wqk_ref, wvt_ref, ws5_ref, lng_ref, lnb_ref, ws_ref, bs_ref,
               ya_ref, q_ref, k_ref, vt_ref, zs_ref, *, width, qk_scale):
    xb = x_ref[...].astype(BF16)
    tm = xb.shape[0]

    z_gmlp = _dot(xb, wgm_ref[...])
    zqk = _dot(xb, wqk_ref[...])
    q_ref[...] = (zqk[:, :width] * qk_scale).astype(BF16)
    k_ref[...] = zqk[:, width:].astype(BF16)
    nt = (((1,), (1,)), ((), ()))
    vt_ref[...] = lax.dot_general(wvt_ref[...], xb, nt, preferred_element_type=F32).astype(BF16)

    zg = jax.nn.gelu(z_gmlp)
    u = zg[:, :width]
    vn = _layer_norm(zg[:, width:], lng_ref[...], lnb_ref[...]).astype(BF16)
    t_idx = lax.broadcasted_iota(jnp.int32, (CHUNK, CHUNK), 0)
    s_idx = lax.broadcasted_iota(jnp.int32, (CHUNK, CHUNK), 1)
    causal = s_idx <= t_idx
    gd = width // GMLP_GROUPS
    for g in range(GMLP_GROUPS):
        ws = jnp.where(causal, ws_ref[g], 0.0).astype(BF16)
        cols = slice(g * gd, (g + 1) * gd)
        for c in range(tm // CHUNK):
            rows = slice(c * CHUNK, (c + 1) * CHUNK)
            s = _dot(ws, vn[rows, cols]) + bs_ref[g]
            ya_ref[rows, cols] = (u[rows, cols] * s).astype(BF16)
    zs_ref[...] = _dot(xb, ws5_ref[...])


def _in_proj(x, w_in, wqk, wvt, lng, lnb, ws, bs, bsz, seq, width, layer):
    n, d = x.shape
    lt = seq // TM
    row = lambda wd_: pl.BlockSpec((TM, wd_), lambda i: (i, 0))
    out_bf = jax.ShapeDtypeStruct((n, width), BF16)
    return pl.pallas_call(
        functools.partial(_in_kernel, width=width, qk_scale=math.log2(math.e) * DIFF_QK_DIM ** -0.5),
        grid=(n // TM,),
        in_specs=[row(d), _layer_spec((d, 2 * width), layer), _const_spec(wqk.shape), _const_spec(wvt.shape),
                  _layer_spec((d, width), layer, col_block=5),
                  _const_spec((1, width)), _const_spec((1, width)), _const_spec(ws.shape),
                  _const_spec(bs.shape)],
        out_specs=[row(width), row(width), row(width),
                   pl.BlockSpec((width, TM), lambda i: (i // lt, i % lt)),
                   row(width)],
        out_shape=[out_bf, out_bf, out_bf,
                   jax.ShapeDtypeStruct((bsz * width, seq), BF16),
                   jax.ShapeDtypeStruct((n, width), F32)],
        compiler_params=_params("parallel"),
        name="in_proj_gmlp",
    )(x, w_in, wqk, wvt, w_in, lng.reshape(1, width), lnb.reshape(1, width), ws, bs)


def _attn_kernel(lam_ref, q_ref, k_ref, vt_ref, g_ref, o_ref, acc1_ref, acc2_ref, *, out_scale):
    q = q_ref[...]
    lane = lax.broadcasted_iota(jnp.int32, q.shape, 1)
    zero = jnp.zeros_like(q)
    q1 = jnp.where(lane < DIFF_QK_DIM, q, zero)
    q2 = jnp.where(lane >= DIFF_QK_DIM, q, zero)
    nt = (((1,), (1,)), ((), ()))

    def softmax_step(s, m, l, mask):
        if mask is not None:
            s = jnp.where(mask, s, -jnp.inf)
        m_new = jnp.maximum(m, jnp.max(s, axis=0, keepdims=True))
        p = jnp.exp2(s - m_new)
        corr = jnp.exp2(m - m_new)
        l_new = corr * l + jnp.sum(p, axis=0, keepdims=True)
        return m_new, l_new, corr, p.astype(BF16)

    hq = TQ // 2
    halves = (slice(0, hq), slice(hq, TQ))
    streams = [(h, qm, acc) for h in (0, 1) for qm, acc in ((q1, acc1_ref), (q2, acc2_ref))]

    def scores(tile, j):
        out = []
        for h, qm, _ in streams:
            tk = hq if (j == tile and h == 0) else TQ
            rows = slice(tile * TQ + h * hq, tile * TQ + (h + 1) * hq)
            out.append(lax.dot_general(k_ref[j * TQ:j * TQ + tk, :], qm[rows], nt,
                                       preferred_element_type=F32))
        return out

    def consume(j, s, stats, diagonal):
        new, corr, p = [], [], []
        for n, (h, _, _) in enumerate(streams):
            m, l = stats[n]
            mask = (causal_lo if h == 0 else causal_hi) if diagonal else None
            m, l, c, pn = softmax_step(s[n], m, l, mask)
            new.append((m, l)); corr.append(c); p.append(pn)
        pv = [_dot(vt_ref[:, j * TQ:j * TQ + pn.shape[0]], pn) for pn in p]
        for n, (h, _, acc) in enumerate(streams):
            acc[:, halves[h]] = corr[n] * acc[:, halves[h]] + pv[n]
        return new

    iota = lambda rows, axis: lax.broadcasted_iota(jnp.int32, (rows, hq), axis)
    causal_lo = iota(hq, 0) <= iota(hq, 1)
    causal_hi = iota(TQ, 0) <= iota(TQ, 1) + hq

    def run(diag):
        acc1_ref[...] = jnp.zeros_like(acc1_ref)
        acc2_ref[...] = jnp.zeros_like(acc2_ref)
        stats = [(jnp.full((1, hq), -jnp.inf, F32), jnp.zeros((1, hq), F32))] * len(streams)
        s_next = scores(diag, 0)
        for j in range(diag + 1):
            s_cur = s_next
            if j < diag:
                s_next = scores(diag, j + 1)
            stats = consume(j, s_cur, stats, j == diag)
        for h in (0, 1):
            l1, l2 = stats[2 * h][1], stats[2 * h + 1][1]
            ot = acc1_ref[:, halves[h]] * (1.0 / l1) - acc2_ref[:, halves[h]] * (lam_ref[0] / l2)
            o = ot.T
            o = o * lax.rsqrt(jnp.mean(o * o, axis=-1, keepdims=True) + LN_EPS) * g_ref[...] * out_scale
            o_ref[diag * TQ + h * hq:diag * TQ + (h + 1) * hq, :] = o.astype(BF16)

    for diag in range(k_ref.shape[0] // TQ):
        run(diag)


def _diff_attention(q, k, vt, lam, subln_g, out_scale, bsz, seq):
    n, width = q.shape
    heads = width // DIFF_V_DIM
    qk_spec = pl.BlockSpec((seq, DIFF_V_DIM), lambda b, h: (b, h))
    vt_spec = pl.BlockSpec((DIFF_V_DIM, seq), lambda b, h: (b * heads + h, 0))
    return pl.pallas_call(
        functools.partial(_attn_kernel, out_scale=out_scale),
        grid=(bsz, heads),
        in_specs=[pl.BlockSpec(memory_space=pltpu.SMEM), qk_spec, qk_spec, vt_spec,
                  pl.BlockSpec((1, DIFF_V_DIM), lambda b, h: (0, 0))],
        out_specs=qk_spec,
        out_shape=jax.ShapeDtypeStruct((n, width), BF16),
        scratch_shapes=[pltpu.VMEM((DIFF_V_DIM, TQ), F32), pltpu.VMEM((DIFF_V_DIM, TQ), F32)],
        compiler_params=_params("parallel", "parallel"),
        name="diff_attention",
    )(lam.reshape(1), q, k, vt, subln_g.reshape(1, DIFF_V_DIM))


def _s5_kernel(u_ref, bcat_ref, ccat_ref, ar_ref, ai_ref, d_ref, gw_ref, gb_ref,
               o_ref, h0_ref, h1_ref, st_ref, y_ref):
    @pl.when(pl.program_id(0) == 0)
    def _():
        st_ref[...] = jnp.zeros_like(st_ref)

    bsz, steps, width = u_ref.shape
    u = u_ref[...].reshape(bsz * steps, width)
    ub = u.astype(BF16)
    n_blk = bcat_ref.shape[0]
    slabs = (h0_ref, h1_ref)
    n_slab = h0_ref.shape[0]
    half = n_slab // 2
    pitch = h0_ref.shape[1] // bsz

    def slab_rows(buf, k, b):
        return slabs[buf].at[k, b * pitch:b * pitch + steps, :]

    def project_in(j):
        bu = _dot(ub[:, j * LANES:(j + 1) * LANES], bcat_ref[j])
        for k in range(n_slab):
            for b in range(bsz):
                slab_rows(j % 2, k, b)[...] = bu[b * steps:(b + 1) * steps, k * LANES:(k + 1) * LANES]

    project_in(0)
    for j in range(n_blk):
        if j + 1 < n_blk:
            project_in(j + 1)
        buf = j % 2
        ar = [ar_ref[j, :, k * LANES:(k + 1) * LANES] for k in range(half)]
        ai = [ai_ref[j, :, k * LANES:(k + 1) * LANES] for k in range(half)]
        hr = [st_ref[j, 0, :, k * LANES:(k + 1) * LANES] for k in range(half)]
        hi = [st_ref[j, 1, :, k * LANES:(k + 1) * LANES] for k in range(half)]
        for t in range(steps):
            at_t = pl.ds(t, bsz, stride=pitch)
            for k in range(half):
                nhr = ar[k] * hr[k] - ai[k] * hi[k] + slabs[buf][k, at_t, :]
                nhi = ar[k] * hi[k] + ai[k] * hr[k] + slabs[buf][half + k, at_t, :]
                slabs[buf][k, at_t, :] = nhr
                slabs[buf][half + k, at_t, :] = nhi
                hr[k], hi[k] = nhr, nhi
        for k in range(half):
            st_ref[j, 0, :, k * LANES:(k + 1) * LANES] = hr[k]
            st_ref[j, 1, :, k * LANES:(k + 1) * LANES] = hi[k]
        h = jnp.concatenate(
            [jnp.concatenate([slab_rows(buf, k, b)[...] for k in range(n_slab)], axis=1) for b in range(bsz)],
            axis=0)
        y_ref[:, j * LANES:(j + 1) * LANES] = _dot(h.astype(BF16), ccat_ref[j])

    y = jax.nn.gelu(y_ref[...] + d_ref[...] * u)
    gate = jax.nn.sigmoid(_dot(y.astype(BF16), gw_ref[...]) + gb_ref[...])
    o_ref[...] = (y * gate).astype(BF16).reshape(bsz, steps, width)


def _s5_params(lam_re, lam_im, log_step, b_re, b_im, c_re, c_im, d, bsz):
    g, p = lam_re.shape
    h = b_re.shape[-1]
    gl = LANES // h
    nb = g // gl
    lr = lam_re.astype(F32)
    li = lam_im.astype(F32)
    step = jnp.exp(log_step.astype(F32))[:, None]
    mag = jnp.exp(lr * step)
    ar = mag * jnp.cos(li * step)
    ai = mag * jnp.sin(li * step)
    den = lr * lr + li * li
    cr = ((ar - 1.0) * lr + ai * li) / den
    ci = (ai * lr - (ar - 1.0) * li) / den
    br = b_re.astype(F32)
    bi = b_im.astype(F32)
    bbar_re = cr[..., None] * br - ci[..., None] * bi
    bbar_im = cr[..., None] * bi + ci[..., None] * br
    eye = jnp.eye(gl, dtype=F32)

    def in_blocks(bb):
        t = bb.transpose(0, 2, 1).reshape(nb, gl, h, p)
        return jnp.einsum('jghp,gk->jghkp', t, eye).reshape(nb, gl * h, gl * p)

    def out_blocks(cc):
        t = cc.astype(F32).transpose(0, 2, 1).reshape(nb, gl, p, h)
        return jnp.einsum('jgph,gk->jgpkh', t, eye).reshape(nb, gl * p, gl * h)

    bcat = jnp.concatenate([in_blocks(bbar_re), in_blocks(bbar_im)], axis=-1).astype(BF16)
    ccat = jnp.concatenate([out_blocks(c_re), -out_blocks(c_im)], axis=1).astype(BF16)
    bro = lambda a: jnp.broadcast_to(a.reshape(nb, 1, gl * p), (nb, bsz, gl * p))
    return bcat, ccat, bro(ar), bro(ai), d.astype(F32).reshape(1, g * h)


def _s5(u, bcat, ccat, ar, ai, d, glu_w, glu_b):
    bsz, seq, width = u.shape
    nb, _, ns2 = bcat.shape
    blk = pl.BlockSpec((bsz, S5_T, width), lambda t: (0, t, 0))
    return pl.pallas_call(
        _s5_kernel,
        grid=(seq // S5_T,),
        in_specs=[blk, _const_spec(bcat.shape), _const_spec(ccat.shape), _const_spec(ar.shape),
                  _const_spec(ai.shape), _const_spec((1, width)), _const_spec((width, width)),
                  _const_spec((1, width))],
        out_specs=blk,
        out_shape=jax.ShapeDtypeStruct((bsz, seq, width), BF16),
        scratch_shapes=[pltpu.VMEM((ns2 // LANES, bsz * S5_PITCH, LANES), F32),
                        pltpu.VMEM((ns2 // LANES, bsz * S5_PITCH, LANES), F32),
                        pltpu.VMEM((nb, 2, bsz, ns2 // 2), F32),
                        pltpu.VMEM((bsz * S5_T, width), F32)],
        compiler_params=_params("arbitrary"),
        name="s5_scan",
    )(u, bcat, ccat, ar, ai, d, glu_w.astype(BF16), glu_b.reshape(1, width))


def _mix_kernel(x_ref, ya_ref, yb_ref, yc_ref, wg_ref, wb_ref, wo_ref, g_ref, b_ref, o_ref, m_ref, *, alpha):
    x = x_ref[...]
    xb = x.astype(BF16)
    d = x.shape[1]
    ys = [y_ref[...] for y_ref in (ya_ref, yb_ref, yc_ref)]
    for c in range(d // TF):
        cols = slice(c * TF, (c + 1) * TF)
        mix = None
        for k, y in enumerate(ys):
            gate = jax.nn.sigmoid(_dot(xb, wg_ref[:, k * d + c * TF:k * d + (c + 1) * TF]))
            term = gate * _dot(y, wb_ref[k, :, cols])
            mix = term if mix is None else mix + term
        m_ref[:, cols] = mix.astype(BF16)
    rg = x.shape[0] // ROW_GROUPS
    for r in range(ROW_GROUPS):
        rows = slice(r * rg, (r + 1) * rg)
        out = _dot(m_ref[rows, :], wo_ref[...])
        o_ref[rows, :] = _layer_norm(alpha * x[rows] + out, g_ref[...], b_ref[...])


def _mix(x, ya, yb, yc, w_in, w_branch, w_out, g, b, alpha, layer):
    n, d = x.shape
    width = ya.shape[1]
    row = lambda wd_: pl.BlockSpec((TM, wd_), lambda i: (i, 0))
    assert w_in.shape[2] == 2 * N_BRANCH * d
    return pl.pallas_call(
        functools.partial(_mix_kernel, alpha=alpha),
        grid=(n // TM,),
        in_specs=[row(d), row(width), row(width), row(width),
                  _layer_spec((d, N_BRANCH * d), layer, col_block=1),
                  _layer_spec(w_branch.shape[1:], layer), _layer_spec((d, d), layer),
                  _const_spec((1, d)), _const_spec((1, d))],
        out_specs=row(d),
        out_shape=jax.ShapeDtypeStruct((n, d), F32),
        scratch_shapes=[pltpu.VMEM((TM, d), BF16)],
        compiler_params=_params("parallel"),
        name="branch_mix",
    )(x, ya, yb, yc, w_in, w_branch, w_out, g.reshape(1, d), b.reshape(1, d))


def _head_major_qk(w_qk):
    d, cols = w_qk.shape
    heads = cols // (4 * DIFF_QK_DIM)
    w = w_qk.reshape(d, 2, 2, heads, DIFF_QK_DIM)
    return w.transpose(0, 1, 3, 2, 4).reshape(d, cols)


def kernel(x, ffn1_gate, ffn1_up, ffn1_down, ln1_g, ln1_b, w_in, gmlp_ln_g, gmlp_ln_b, gmlp_ws, gmlp_bs, diff_lq1, diff_lk1, diff_lq2, diff_lk2, diff_subln_g, s5_lambda_re, s5_lambda_im, s5_log_step, s5_b_re, s5_b_im, s5_c_re, s5_c_im, s5_d, s5_glu_w, s5_glu_b, w_branch, w_out, ln2_g, ln2_b, ffn2_gate, ffn2_up, ffn2_down, ln3_g, ln3_b):
    bsz, seq, d = x.shape
    depth = w_in.shape[0]
    width = w_branch.shape[2]
    alpha = (2 * depth) ** 0.25
    n = bsz * seq
    xf = x.reshape(n, d)
    cast = lambda a: a.astype(BF16)
    ffn1 = (cast(ffn1_gate), cast(ffn1_up), cast(ffn1_down))
    ffn2 = (cast(ffn2_gate), cast(ffn2_up), cast(ffn2_down))
    w_in_b, w_branch_b, w_out_b = cast(w_in), cast(w_branch), cast(w_out)
    for i in range(depth):
        xf = _ffn(xf, *ffn1, ln1_g[i], ln1_b[i], alpha, i)

        w_qk = _head_major_qk(w_in_b[i, :, 2 * width:4 * width])
        w_vt = w_in_b[i, :, 4 * width:5 * width].T
        bs_full = jnp.broadcast_to(gmlp_bs[i][:, :, None], (GMLP_GROUPS, CHUNK, width // GMLP_GROUPS))
        ya, q, k, vt, zs = _in_proj(xf, w_in_b, w_qk, w_vt, gmlp_ln_g[i], gmlp_ln_b[i], gmlp_ws[i], bs_full,
                                    bsz, seq, width, i)

        lam_init = 0.8 - 0.6 * math.exp(-0.3 * i)
        lam = (jnp.exp(jnp.sum(diff_lq1[i].astype(F32) * diff_lk1[i].astype(F32)))
               - jnp.exp(jnp.sum(diff_lq2[i].astype(F32) * diff_lk2[i].astype(F32))) + lam_init)
        yb = _diff_attention(q, k, vt, lam, diff_subln_g[i], 1.0 - lam_init, bsz, seq)

        s5p = _s5_params(s5_lambda_re[i], s5_lambda_im[i], s5_log_step[i], s5_b_re[i], s5_b_im[i],
                         s5_c_re[i], s5_c_im[i], s5_d[i], bsz)
        yc = _s5(zs.reshape(bsz, seq, width), *s5p, s5_glu_w[i], s5_glu_b[i])

        xf = _mix(xf, ya, yb, yc.reshape(n, width), w_in_b, w_branch_b, w_out_b,
                  ln2_g[i], ln2_b[i], alpha, i)

        xf = _ffn(xf, *ffn2, ln3_g[i], ln3_b[i], alpha, i)
    return xf.reshape(bsz, seq, d)
```
